```python
import math
import jax
import jax.numpy as jnp
from jax import lax
import numpy as np

D_MODEL = 1024
BATCH = 2
SEQ = 8192
DEPTH = 2
DEC_BATCH = 4
DEC_SEQ = 4096
PAST_LEN = 128

GRID_W = 64
N_MEM = 256
D_FF = 2816
HEAD_DIM = 64
BRANCH_W = 256
N_BRANCH = 5
SSM_P = 16
SSM_G = BRANCH_W // SSM_P
SSM_N = 64
SWA_HQ = 4
SWA_HKV = 2
SWA_WIN = 128
SWA_BLK = 128
T5_BUCKETS = 32
T5_MAX_DIST = 128
NA_H = 4
NA_KH = 8
NA_KW = 16
MLA_H = 4
MLA_Q_RANK = 192
MLA_KV_RANK = 128
MLA_NOPE = 64
MLA_ROPE = 32
MLA_V = 64
MLA_BLK = 128
ROPE_THETA = 10000.0
MEM_H = 4
EPS = 1e-6
NEG = -1e30
IN_WIDTHS = (BRANCH_W,
             SWA_HQ * HEAD_DIM, SWA_HKV * HEAD_DIM, SWA_HKV * HEAD_DIM,
             NA_H * HEAD_DIM, NA_H * HEAD_DIM, NA_H * HEAD_DIM,
             MLA_Q_RANK, MLA_KV_RANK, MLA_ROPE,
             MEM_H * HEAD_DIM,
             N_BRANCH * D_MODEL)
D_IN = sum(IN_WIDTHS)

kernel_name = 'hybrid_bidir_encoder_two_groups'

F32 = jnp.float32


def _rmsnorm(x, g):
    xf = x.astype(F32)
    y = xf * lax.rsqrt(jnp.mean(xf * xf, axis=-1, keepdims=True) + EPS)
    return (y * g.astype(F32)).astype(x.dtype)


def _swiglu(x, w_gate, w_up, w_down):
    return (jax.nn.silu(x @ w_gate) * (x @ w_up)) @ w_down


def _cplx_combine(e1, e2):
    a1r, a1i, b1r, b1i = e1
    a2r, a2i, b2r, b2i = e2
    return (a2r * a1r - a2i * a1i,
            a2r * a1i + a2i * a1r,
            a2r * b1r - a2i * b1i + b2r,
            a2r * b1i + a2i * b1r + b2i)


def _s5(u, lam_re, lam_im, log_step, b_re, b_im, c_re, c_im, d_skip, w_glu):
    Bsz, S, _ = u.shape
    uf = u.astype(F32).reshape(Bsz, S, SSM_G, SSM_P)
    lr = lam_re.astype(F32)
    li = lam_im.astype(F32)
    dt = jnp.exp(log_step.astype(F32))[..., None]
    mag = jnp.exp(lr * dt)
    a_re = mag * jnp.cos(li * dt)
    a_im = mag * jnp.sin(li * dt)
    den = lr * lr + li * li
    xr = a_re - 1.0
    k_re = (xr * lr + a_im * li) / den
    k_im = (a_im * lr - xr * li) / den
    br = b_re.astype(F32)
    bi = b_im.astype(F32)
    bb_re = k_re[..., None] * br - k_im[..., None] * bi
    bb_im = k_re[..., None] * bi + k_im[..., None] * br
    cr = c_re.astype(F32)
    ci = c_im.astype(F32)
    y = d_skip.astype(F32).reshape(SSM_G, SSM_P) * uf
    for dirn, rev in ((0, False), (1, True)):
        bu_re = jnp.einsum('bsgp,gnp->bsgn', uf, bb_re[dirn])
        bu_im = jnp.einsum('bsgp,gnp->bsgn', uf, bb_im[dirn])
        ar = jnp.broadcast_to(a_re[dirn], bu_re.shape)
        ai = jnp.broadcast_to(a_im[dirn], bu_re.shape)
        _, _, st_re, st_im = lax.associative_scan(_cplx_combine, (ar, ai, bu_re, bu_im),
                                                  reverse=rev, axis=1)
        y = y + jnp.einsum('bsgn,gpn->bsgp', st_re, cr[dirn]) \
              - jnp.einsum('bsgn,gpn->bsgp', st_im, ci[dirn])
    g = jax.nn.gelu(y.reshape(Bsz, S, BRANCH_W)).astype(u.dtype)
    return g * jax.nn.sigmoid(g @ w_glu)


def _t5_bucket(rel):
    nb = T5_BUCKETS // 2
    max_exact = nb // 2
    ret = (rel > 0).astype(jnp.int32) * nb
    n = jnp.abs(rel)
    nf = jnp.maximum(n, 1).astype(F32)
    large = max_exact + (jnp.log(nf / max_exact) / math.log(T5_MAX_DIST / max_exact)
                         * (nb - max_exact)).astype(jnp.int32)
    large = jnp.minimum(large, nb - 1)
    return ret + jnp.where(n < max_exact, n, large)


def _swa(q, k, v, sink, t5_bias):
    Bsz, S, _ = q.shape
    nb = S // SWA_BLK
    rep = SWA_HQ // SWA_HKV
    qb = q.reshape(Bsz, nb, SWA_BLK, SWA_HKV, rep, HEAD_DIM)
    pad = ((0, 0), (SWA_BLK, SWA_BLK), (0, 0), (0, 0))
    kp = jnp.pad(k.reshape(Bsz, S, SWA_HKV, HEAD_DIM), pad).reshape(Bsz, nb + 2, SWA_BLK, SWA_HKV, HEAD_DIM)
    vp = jnp.pad(v.reshape(Bsz, S, SWA_HKV, HEAD_DIM), pad).reshape(Bsz, nb + 2, SWA_BLK, SWA_HKV, HEAD_DIM)
    kb = jnp.concatenate([kp[:, :-2], kp[:, 1:-1], kp[:, 2:]], axis=2)
    vb = jnp.concatenate([vp[:, :-2], vp[:, 1:-1], vp[:, 2:]], axis=2)
    logits = jnp.einsum('bnqgrd,bnkgd->bngrqk', qb, kb,
                        preferred_element_type=F32) * (HEAD_DIM ** -0.5)
    qi = jnp.arange(SWA_BLK)[:, None]
    kj = jnp.arange(3 * SWA_BLK)[None, :]
    rel = kj - SWA_BLK - qi
    bias = t5_bias.astype(F32)[_t5_bucket(rel)]
    bias = bias.transpose(2, 0, 1).reshape(SWA_HKV, rep, SWA_BLK, 3 * SWA_BLK)
    kpos = jnp.arange(nb)[:, None] * SWA_BLK + kj - SWA_BLK
    valid = (jnp.abs(rel) <= SWA_WIN)[None] & ((kpos >= 0) & (kpos < S))[:, None, :]
    logits = jnp.where(valid[None, :, None, None], logits + bias, NEG)
    sink_col = jnp.broadcast_to(sink.astype(F32).reshape(SWA_HKV, rep)[None, None, :, :, None, None],
                                logits.shape[:-1] + (1,))
    p = jax.nn.softmax(jnp.concatenate([logits, sink_col], axis=-1), axis=-1)[..., :-1]
    out = jnp.einsum('bngrqk,bnkgd->bnqgrd', p.astype(v.dtype), vb)
    return out.reshape(Bsz, S, SWA_HQ * HEAD_DIM)


def _na(q, k, v, rpb):
    Bsz, S, _ = q.shape
    rows = S // GRID_W
    kh = min(NA_KH, rows)
    kw = NA_KW
    qg = q.reshape(Bsz, rows, GRID_W, NA_H, HEAD_DIM)
    kg = k.reshape(Bsz, rows, GRID_W, NA_H, HEAD_DIM)
    vg = v.reshape(Bsz, rows, GRID_W, NA_H, HEAD_DIM)
    cols = jnp.arange(GRID_W)
    cs = jnp.clip(cols - kw // 2, 0, GRID_W - kw)
    col_idx = cs[:, None] + jnp.arange(kw)[None, :]
    dc_idx = col_idx - cols[:, None] + (NA_KW - 1)
    rpb_f = rpb.astype(F32)

    def row_fn(args):
        r, q_row = args
        rs = jnp.clip(r - kh // 2, 0, rows - kh)
        k_rows = lax.dynamic_slice_in_dim(kg, rs, kh, axis=1)
        v_rows = lax.dynamic_slice_in_dim(vg, rs, kh, axis=1)
        k_win = k_rows[:, :, col_idx]
        v_win = v_rows[:, :, col_idx]
        logits = jnp.einsum('bchd,brckhd->bhcrk', q_row, k_win,
                            preferred_element_type=F32) * (HEAD_DIM ** -0.5)
        dr_idx = rs + jnp.arange(kh) - r + (NA_KH - 1)
        bias = rpb_f[:, dr_idx[None, :, None], dc_idx[:, None, :]]
        logits = logits + bias[None]
        p = jax.nn.softmax(logits.reshape(Bsz, NA_H, GRID_W, kh * kw), axis=-1)
        p = p.reshape(Bsz, NA_H, GRID_W, kh, kw).astype(v.dtype)
        return jnp.einsum('bhcrk,brckhd->bchd', p, v_win)

    out = lax.map(row_fn, (jnp.arange(rows), qg.transpose(1, 0, 2, 3, 4)))
    return out.transpose(1, 0, 2, 3, 4).reshape(Bsz, S, NA_H * HEAD_DIM)


def _rope(x, pos):
    half = x.shape[-1] // 2
    inv = ROPE_THETA ** (-jnp.arange(half, dtype=F32) / half)
    ang = pos.astype(F32)[:, None] * inv[None, :]
    cos = jnp.cos(ang)[:, None, :]
    sin = jnp.sin(ang)[:, None, :]
    x1 = x[..., :half].astype(F32)
    x2 = x[..., half:].astype(F32)
    return jnp.concatenate([x1 * cos - x2 * sin, x1 * sin + x2 * cos], axis=-1).astype(x.dtype)


def _mla(c_q, c_kv, k_rope, q_norm, w_q_up, kv_norm, w_kv_up):
    Bsz, S, _ = c_q.shape
    dq = MLA_NOPE + MLA_ROPE
    pos = jnp.arange(S)
    q = (_rmsnorm(c_q, q_norm) @ w_q_up).reshape(Bsz, S, MLA_H, dq)
    kv = (_rmsnorm(c_kv, kv_norm) @ w_kv_up).reshape(Bsz, S, MLA_H, MLA_NOPE + MLA_V)
    q = jnp.concatenate([q[..., :MLA_NOPE], _rope(q[..., MLA_NOPE:], pos)], axis=-1)
    kr = jnp.broadcast_to(_rope(k_rope[:, :, None, :], pos), (Bsz, S, MLA_H, MLA_ROPE))
    k = jnp.concatenate([kv[..., :MLA_NOPE], kr], axis=-1)
    v = kv[..., MLA_NOPE:]
    nb = S // MLA_BLK
    qb = q.reshape(Bsz, nb, MLA_BLK, MLA_H, dq).transpose(1, 0, 2, 3, 4)

    def blk(q_blk):
        logits = jnp.einsum('bqhd,bkhd->bhqk', q_blk, k, preferred_element_type=F32) * (dq ** -0.5)
        p = jax.nn.softmax(logits, axis=-1).astype(v.dtype)
        return jnp.einsum('bhqk,bkhd->bqhd', p, v)

    out = lax.map(blk, qb)
    return out.transpose(1, 0, 2, 3, 4).reshape(Bsz, S, MLA_H * MLA_V)


def _mem_attn(q, mem, mem_norm, w_mem_kv):
    Bsz, S, _ = q.shape
    kv = _rmsnorm(mem, mem_norm) @ w_mem_kv
    k, v = jnp.split(kv, 2, axis=-1)
    k = k.reshape(Bsz, -1, MEM_H, HEAD_DIM)
    v = v.reshape(Bsz, -1, MEM_H, HEAD_DIM)
    qh = q.reshape(Bsz, S, MEM_H, HEAD_DIM)
    logits = jnp.einsum('bshd,bmhd->bhsm', qh, k, preferred_element_type=F32) * (HEAD_DIM ** -0.5)
    p = jax.nn.softmax(logits, axis=-1).astype(v.dtype)
    return jnp.einsum('bhsm,bmhd->bshd', p, v).reshape(Bsz, S, MEM_H * HEAD_DIM)


def _trunk(x, mem, w):
    Bsz, S, _ = x.shape
    offs = np.cumsum(IN_WIDTHS)[:-1].tolist()
    h = x
    for l in range(DEPTH):
        h = h + 0.5 * _swiglu(_rmsnorm(h, w['ffn1_norm'][l]), w['ffn1_w_gate'][l],
                              w['ffn1_w_up'][l], w['ffn1_w_down'][l])
        u = _rmsnorm(h, w['mix_norm'][l])
        (a_in, swa_q, swa_k, swa_v, na_q, na_k, na_v, c_q, c_kv, k_rope, mem_q,
         gate_logits) = jnp.split(u @ w['w_in'][l], offs, axis=-1)
        branches = (
            _s5(a_in, w['ssm_lam_re'][l], w['ssm_lam_im'][l], w['ssm_log_step'][l],
                w['ssm_b_re'][l], w['ssm_b_im'][l], w['ssm_c_re'][l], w['ssm_c_im'][l],
                w['ssm_d'][l], w['ssm_w_glu'][l]),
            _swa(swa_q, swa_k, swa_v, w['swa_sink'][l], w['t5_bias']),
            _na(na_q, na_k, na_v, w['na_rpb'][l]),
            _mla(c_q, c_kv, k_rope, w['mla_q_norm'][l], w['mla_w_q_up'][l],
                 w['mla_kv_norm'][l], w['mla_w_kv_up'][l]),
            _mem_attn(mem_q, mem, w['mem_norm'][l], w['mem_w_kv'][l]),
        )
        gates = jax.nn.sigmoid(gate_logits.astype(F32)).astype(h.dtype).reshape(Bsz, S, N_BRANCH, D_MODEL)
        merged = gates[:, :, 0] * (branches[0] @ w['w_branch'][l, 0])
        for n in range(1, N_BRANCH):
            merged = merged + gates[:, :, n] * (branches[n] @ w['w_branch'][l, n])
        h = h + merged @ w['w_out'][l]
        h = h + 0.5 * _swiglu(_rmsnorm(h, w['ffn2_norm'][l]), w['ffn2_w_gate'][l],
                              w['ffn2_w_up'][l], w['ffn2_w_down'][l])
    return _rmsnorm(h, w['final_norm'])


def setup_inputs(seed: int = 0) -> dict:
    key = jax.random.key(seed)
    ks = iter(jax.random.split(key, 48))

    def nrm(shape, scale):
        return jax.random.normal(next(ks), shape, F32) * scale

    def gain(shape):
        return 1.0 + 0.01 * jax.random.normal(next(ks), shape, F32)

    L, G, N, P = DEPTH, SSM_G, SSM_N, SSM_P
    return {
        'x_prompt': nrm((BATCH, SEQ, D_MODEL), 1.0),
        'x_sample': nrm((DEC_BATCH, DEC_SEQ, D_MODEL), 1.0),
        'mem_prompt': nrm((BATCH, N_MEM, D_MODEL), 1.0),
        'mem_sample': nrm((DEC_BATCH, N_MEM, D_MODEL), 1.0),
        'ffn1_norm': gain((L, D_MODEL)),
        'ffn1_w_gate': nrm((L, D_MODEL, D_FF), D_MODEL ** -0.5),
        'ffn1_w_up': nrm((L, D_MODEL, D_FF), D_MODEL ** -0.5),
        'ffn1_w_down': nrm((L, D_FF, D_MODEL), D_FF ** -0.5),
        'mix_norm': gain((L, D_MODEL)),
        'w_in': nrm((L, D_MODEL, D_IN), D_MODEL ** -0.5),
        'ssm_lam_re': -0.5 + nrm((L, 2, G, N), 0.01),
        'ssm_lam_im': jnp.pi * jnp.arange(N, dtype=F32) + nrm((L, 2, G, N), 0.01),
        'ssm_log_step': jax.random.uniform(next(ks), (L, 2, G), F32, math.log(1e-3), math.log(1e-1)),
        'ssm_b_re': nrm((L, 2, G, N, P), (2 * P) ** -0.5),
        'ssm_b_im': nrm((L, 2, G, N, P), (2 * P) ** -0.5),
        'ssm_c_re': nrm((L, 2, G, P, N), (2 * N) ** -0.5),
        'ssm_c_im': nrm((L, 2, G, P, N), (2 * N) ** -0.5),
        'ssm_d': nrm((L, BRANCH_W), 1.0),
        'ssm_w_glu': nrm((L, BRANCH_W, BRANCH_W), BRANCH_W ** -0.5),
        'swa_sink': nrm((L, SWA_HQ), 0.5),
        't5_bias': nrm((T5_BUCKETS, SWA_HQ), 0.1),
        'na_rpb': nrm((L, NA_H, 2 * NA_KH - 1, 2 * NA_KW - 1), 0.02),
        'mla_q_norm': gain((L, MLA_Q_RANK)),
        'mla_w_q_up': nrm((L, MLA_Q_RANK, MLA_H * (MLA_NOPE + MLA_ROPE)), MLA_Q_RANK ** -0.5),
        'mla_kv_norm': gain((L, MLA_KV_RANK)),
        'mla_w_kv_up': nrm((L, MLA_KV_RANK, MLA_H * (MLA_NOPE + MLA_V)), MLA_KV_RANK ** -0.5),
        'mem_norm': gain((L, D_MODEL)),
        'mem_w_kv': nrm((L, D_MODEL, 2 * MEM_H * HEAD_DIM), D_MODEL ** -0.5),
        'w_branch': nrm((L, N_BRANCH, BRANCH_W, D_MODEL), BRANCH_W ** -0.5),
        'w_out': nrm((L, D_MODEL, D_MODEL), D_MODEL ** -0.5),
        'ffn2_norm': gain((L, D_MODEL)),
        'ffn2_w_gate': nrm((L, D_MODEL, D_FF), D_MODEL ** -0.5),
        'ffn2_w_up': nrm((L, D_MODEL, D_FF), D_MODEL ** -0.5),
        'ffn2_w_down': nrm((L, D_FF, D_MODEL), D_FF ** -0.5),
        'final_norm': gain((D_MODEL,)),
    }


def reference(x_prompt, x_sample, mem_prompt, mem_sample,
              ffn1_norm, ffn1_w_gate, ffn1_w_up, ffn1_w_down,
              mix_norm, w_in,
              ssm_lam_re, ssm_lam_im, ssm_log_step, ssm_b_re, ssm_b_im, ssm_c_re, ssm_c_im,
              ssm_d, ssm_w_glu,
              swa_sink, t5_bias, na_rpb,
              mla_q_norm, mla_w_q_up, mla_kv_norm, mla_w_kv_up,
              mem_norm, mem_w_kv, w_branch, w_out,
              ffn2_norm, ffn2_w_gate, ffn2_w_up, ffn2_w_down,
              final_norm):
    w = dict(ffn1_norm=ffn1_norm, ffn1_w_gate=ffn1_w_gate, ffn1_w_up=ffn1_w_up, ffn1_w_down=ffn1_w_down,
             mix_norm=mix_norm, w_in=w_in,
             ssm_lam_re=ssm_lam_re, ssm_lam_im=ssm_lam_im, ssm_log_step=ssm_log_step,
             ssm_b_re=ssm_b_re, ssm_b_im=ssm_b_im, ssm_c_re=ssm_c_re, ssm_c_im=ssm_c_im,
             ssm_d=ssm_d, ssm_w_glu=ssm_w_glu,
             swa_sink=swa_sink, t5_bias=t5_bias, na_rpb=na_rpb,
             mla_q_norm=mla_q_norm, mla_w_q_up=mla_w_q_up, mla_kv_norm=mla_kv_norm, mla_w_kv_up=mla_w_kv_up,
             mem_norm=mem_norm, mem_w_kv=mem_w_kv, w_branch=w_branch, w_out=w_out,
             ffn2_norm=ffn2_norm, ffn2_w_gate=ffn2_w_gate, ffn2_w_up=ffn2_w_up, ffn2_w_down=ffn2_w_down,
             final_norm=final_norm)
    y_prompt = _trunk(x_prompt, mem_prompt, w)
    y_sample = _trunk(x_sample, mem_sample, w)
    return (y_prompt, y_sample)
```

```python
import functools
import math

import numpy as np
import jax
import jax.numpy as jnp
from jax import lax
from jax.experimental import pallas as pl
from jax.experimental.pallas import tpu as pltpu

F32 = jnp.float32
BF16 = jnp.bfloat16

D_MODEL = 1024
DEPTH = 2
GRID_W = 64
D_FF = 2816
HEAD_DIM = 64
BRANCH_W = 256
N_BRANCH = 5
SSM_P = 16
SSM_G = BRANCH_W // SSM_P
SSM_N = 64
SWA_HQ = 4
SWA_HKV = 2
SWA_WIN = 128
SWA_BLK = 128
T5_BUCKETS = 32
T5_MAX_DIST = 128
NA_H = 4
NA_KH = 8
NA_KW = 16
MLA_H = 4
MLA_Q_RANK = 192
MLA_KV_RANK = 128
MLA_NOPE = 64
MLA_ROPE = 32
MLA_V = 64
ROPE_THETA = 10000.0
MEM_H = 4
EPS = 1e-6
NEG = -1e30

LANES = 128
HALF = LANES // 2
SSM_T = 64
SSM_TP = SSM_T * SSM_P
FFN_CHUNK = 256
TOKEN_TILE = 512
MLA_TQ = 512
MLA_TK = 512
NA_ROWS_PER_STEP = 8
VMEM_LIMIT = 56 * 1024 * 1024
MLA_QSCALE = (MLA_NOPE + MLA_ROPE) ** -0.5 * math.log2(math.e)

IN_A = (0, 256)
IN_SWA = (256, 1024)
IN_NA = (1024, 1792)
IN_MEMQ = (1792, 2048)
IN_LAT = (2048, 2688)
IN_COLS = 2688


def _dot(a, b):
    return jnp.dot(a, b, preferred_element_type=F32)


def _dot_nt(a, b):
    return lax.dot_general(a, b, (((1,), (1,)), ((), ())), preferred_element_type=F32)


def _rms(x, g):
    return x * lax.rsqrt(jnp.mean(x * x, axis=-1, keepdims=True) + EPS) * g


def _sigmoid(x):
    return 1.0 / (1.0 + jnp.exp(-x))


def _params(sem):
    return pltpu.CompilerParams(dimension_semantics=sem, vmem_limit_bytes=VMEM_LIMIT)


def _const_spec(shape):
    nd = len(shape)
    return pl.BlockSpec(shape, lambda *_: (0,) * nd, pipeline_mode=pl.Buffered(1))


def _ffn_kernel(*refs, final):
    if final:
        x_ref, g_ref, wg_ref, wu_ref, wd_ref, fn_ref, o_ref = refs
    else:
        x_ref, g_ref, wg_ref, wu_ref, wd_ref, o_ref = refs
    x = x_ref[...]
    xn = _rms(x, g_ref[...]).astype(BF16)
    acc = jnp.zeros(x.shape, F32)
    for c in range(D_FF // FFN_CHUNK):
        sl = slice(c * FFN_CHUNK, (c + 1) * FFN_CHUNK)
        g = _dot(xn, wg_ref[:, sl])
        u = _dot(xn, wu_ref[:, sl])
        a = (g * _sigmoid(g) * u).astype(BF16)
        acc = acc + _dot(a, wd_ref[sl, :])
    y = x + 0.5 * acc
    if final:
        y = _rms(y, fn_ref[...])
    o_ref[...] = y


def _ffn(x, g, wg, wu, wd, final_g=None):
    n = x.shape[0]
    tm = TOKEN_TILE
    final = final_g is not None
    ins = [x, g, wg, wu, wd]
    specs = [pl.BlockSpec((tm, D_MODEL), lambda i: (i, 0)),
             _const_spec((1, D_MODEL)),
             _const_spec((D_MODEL, D_FF)), _const_spec((D_MODEL, D_FF)),
             _const_spec((D_FF, D_MODEL))]
    if final:
        ins.append(final_g)
        specs.append(_const_spec((1, D_MODEL)))
    return pl.pallas_call(
        functools.partial(_ffn_kernel, final=final),
        grid=(n // tm,),
        in_specs=specs,
        out_specs=pl.BlockSpec((tm, D_MODEL), lambda i: (i, 0)),
        out_shape=jax.ShapeDtypeStruct((n, D_MODEL), F32),
        compiler_params=_params(("parallel",)),
        name="ffn_final" if final else "ffn",
    )(*ins)


def _inproj_kernel(x_ref, g_ref, w_ref, wq_ref, wkv_ref, qn_ref, kvn_ref, cos_ref, sin_ref,
                   a_ref, swa_ref, na_ref, mq_ref, q_ref, k_ref, v_ref):
    u = _rms(x_ref[...], g_ref[...]).astype(BF16)
    a_ref[...] = _dot(u, w_ref[:, IN_A[0]:IN_A[1]])
    swa_ref[...] = _dot(u, w_ref[:, IN_SWA[0]:IN_SWA[1]]).astype(BF16)
    na_ref[...] = _dot(u, w_ref[:, IN_NA[0]:IN_NA[1]]).astype(BF16)
    mq_ref[...] = _dot(u, w_ref[:, IN_MEMQ[0]:IN_MEMQ[1]]).astype(BF16)
    lat = _dot(u, w_ref[:, IN_LAT[0]:IN_LAT[1]])
    cq = lat[:, 0:256]
    ckv = lat[:, 256:384]
    kra = lat[:, 384:512]
    krb = lat[:, 512:640]
    cos = cos_ref[...]
    sin = sin_ref[...]
    ms = jnp.sum(cq * cq, axis=-1, keepdims=True) * (1.0 / MLA_Q_RANK)
    cqn = (cq * lax.rsqrt(ms + EPS) * qn_ref[...]).astype(BF16)
    qab = _dot(cqn, wq_ref[...])
    cos4 = jnp.concatenate([cos] * MLA_H, axis=1)
    sin4 = jnp.concatenate([sin] * MLA_H, axis=1)
    nq = MLA_H * LANES
    q = (qab[:, :nq] * cos4 + qab[:, nq:] * sin4) * MLA_QSCALE
    q_ref[...] = q.astype(BF16)
    ckvn = _rms(ckv, kvn_ref[...]).astype(BF16)
    kv = _dot(ckvn, wkv_ref[...])
    kr = kra * cos + krb * sin
    k_ref[...] = (kv[:, :nq] + jnp.concatenate([kr] * MLA_H, axis=1)).astype(BF16)
    v_ref[...] = kv[:, nq:].astype(BF16)


def _inproj(x, seq_len, g, w, wq, wkv, qn, kvn, cos_t, sin_t):
    n = x.shape[0]
    tm = TOKEN_TILE
    ns = seq_len // tm
    widths = (256, 768, 768, 256, 512, 512, 256)
    dtypes = (F32, BF16, BF16, BF16, BF16, BF16, BF16)
    return pl.pallas_call(
        _inproj_kernel,
        grid=(n // tm,),
        in_specs=[pl.BlockSpec((tm, D_MODEL), lambda i: (i, 0)),
                  _const_spec((1, D_MODEL)),
                  _const_spec((D_MODEL, IN_COLS)),
                  _const_spec(wq.shape), _const_spec(wkv.shape),
                  _const_spec((1, 256)), _const_spec((1, MLA_KV_RANK)),
                  pl.BlockSpec((tm, LANES), lambda i: (i % ns, 0)),
                  pl.BlockSpec((tm, LANES), lambda i: (i % ns, 0))],
        out_specs=[pl.BlockSpec((tm, wd), lambda i: (i, 0)) for wd in widths],
        out_shape=[jax.ShapeDtypeStruct((n, wd), dt) for wd, dt in zip(widths, dtypes)],
        compiler_params=_params(("parallel",)),
        name="inproj",
    )(x, g, w, wq, wkv, qn, kvn, cos_t, sin_t)


def _ssm_conv_kernel(u_ref, mt_ref, win_ref, wout_ref, ap_ref, y_ref, s_scr, x_scr, *, segments):
    u = u_ref[...]
    s_scr[...] = _dot(u, win_ref[...])
    afr = ap_ref[0:1, :]
    afi = ap_ref[1:2, :]
    abr = ap_ref[2:3, :]
    abi = ap_ref[3:4, :]
    z = jnp.zeros((1, LANES), F32)
    sub = 8
    for base, nch in segments:
        def body(c, carry, base=base, nch=nch):
            xfr, xfi, xbr, xbi = carry
            rf = pl.multiple_of(base + c * sub, sub)
            rb = pl.multiple_of(base + nch - (c + 1) * sub, sub)
            sf = s_scr[pl.ds(rf, sub), 0:256]
            sb = s_scr[pl.ds(rb, sub), 256:512]
            ofr, ofi, obr, obi = [], [], [None] * sub, [None] * sub
            for k in range(sub):
                ofr.append(xfr)
                ofi.append(xfi)
                xfr, xfi = (afr * xfr - afi * xfi + sf[k:k + 1, 0:128],
                            afr * xfi + afi * xfr + sf[k:k + 1, 128:256])
                kb = sub - 1 - k
                obr[kb] = xbr
                obi[kb] = xbi
                xbr, xbi = (abr * xbr - abi * xbi + sb[kb:kb + 1, 0:128],
                            abr * xbi + abi * xbr + sb[kb:kb + 1, 128:256])
            x_scr[pl.ds(rf, sub), 0:128] = jnp.concatenate(ofr, axis=0)
            x_scr[pl.ds(rf, sub), 128:256] = jnp.concatenate(ofi, axis=0)
            x_scr[pl.ds(rb, sub), 256:384] = jnp.concatenate(obr, axis=0)
            x_scr[pl.ds(rb, sub), 384:512] = jnp.concatenate(obi, axis=0)
            return xfr, xfi, xbr, xbi
        assert base % sub == 0 and nch % sub == 0
        lax.fori_loop(0, nch // sub, body, (z, z, z, z))
    y_ref[...] = _dot(u, mt_ref[...]) + _dot(x_scr[...].astype(BF16), wout_ref[...])


def _ssm_conv(u_g, mt, win, wout, apow, segments):
    g, nc, tp = u_g.shape
    return pl.pallas_call(
        functools.partial(_ssm_conv_kernel, segments=segments),
        grid=(g,),
        in_specs=[pl.BlockSpec((None, nc, tp), lambda i: (i, 0, 0)),
                  pl.BlockSpec((None, tp, tp), lambda i: (i, 0, 0)),
                  pl.BlockSpec((None, tp, 512), lambda i: (i, 0, 0)),
                  pl.BlockSpec((None, 512, tp), lambda i: (i, 0, 0)),
                  pl.BlockSpec((None, 4, LANES), lambda i: (i, 0, 0))],
        out_specs=pl.BlockSpec((None, nc, tp), lambda i: (i, 0, 0)),
        out_shape=jax.ShapeDtypeStruct((g, nc, tp), F32),
        scratch_shapes=[pltpu.VMEM((nc, 512), F32), pltpu.VMEM((nc, 512), F32)],
        compiler_params=_params(("parallel",)),
        name="ssm_conv",
    )(u_g, mt, win, wout, apow)


def _ssm_post_kernel(a_ref, y_ref, d_ref, w_ref, o_ref):
    y = d_ref[...] * a_ref[...] + y_ref[...]
    c = math.sqrt(2.0 / math.pi)
    gl = y * (0.5 * (1.0 + jnp.tanh(c * (y + 0.044715 * (y * y * y)))))
    o_ref[...] = (gl * _sigmoid(_dot(gl.astype(BF16), w_ref[...]))).astype(BF16)


def _ssm_post(a, y, d, w):
    n = a.shape[0]
    tm = 1024
    return pl.pallas_call(
        _ssm_post_kernel,
        grid=(n // tm,),
        in_specs=[pl.BlockSpec((tm, BRANCH_W), lambda i: (i, 0)),
                  pl.BlockSpec((tm, BRANCH_W), lambda i: (i, 0)),
                  _const_spec((1, BRANCH_W)), _const_spec((BRANCH_W, BRANCH_W))],
        out_specs=pl.BlockSpec((tm, BRANCH_W), lambda i: (i, 0)),
        out_shape=jax.ShapeDtypeStruct((n, BRANCH_W), BF16),
        compiler_params=_params(("parallel",)),
        name="ssm_post",
    )(a, y, d, w)


def _ssm_tables(lam_re, lam_im, log_step, b_re, b_im, c_re, c_im):
    hp = lax.Precision.HIGHEST
    t, g, n, p = SSM_T, SSM_G, SSM_N, SSM_P
    dt = jnp.exp(log_step)[..., None]
    lr, li = lam_re, lam_im
    mag = jnp.exp(lr * dt)
    a_re = mag * jnp.cos(li * dt)
    a_im = mag * jnp.sin(li * dt)
    den = lr * lr + li * li
    xr = a_re - 1.0
    k_re = (xr * lr + a_im * li) / den
    k_im = (a_im * lr - xr * li) / den
    bb_re = k_re[..., None] * b_re - k_im[..., None] * b_im
    bb_im = k_re[..., None] * b_im + k_im[..., None] * b_re
    kk = jnp.arange(t + 1, dtype=F32)[:, None, None, None]
    pmag = jnp.exp(kk * (lr * dt)[None])
    ang = kk * (li * dt)[None]
    pw_re = pmag * jnp.cos(ang)
    pw_im = pmag * jnp.sin(ang)
    e_re = pw_re[..., None] * bb_re[None] - pw_im[..., None] * bb_im[None]
    e_im = pw_re[..., None] * bb_im[None] + pw_im[..., None] * bb_re[None]
    kd = (jnp.einsum('dgpn,tdgnq->dtgpq', c_re, e_re[:t], precision=hp)
          - jnp.einsum('dgpn,tdgnq->dtgpq', c_im, e_im[:t], precision=hp))
    kf, kb = kd[0], kd[1]
    kfull = jnp.concatenate([kb[:0:-1], (kf[0] + kb[0])[None], kf[1:]], axis=0)
    idx = np.arange(t)[None, :] - np.arange(t)[:, None] + (t - 1)
    mt = kfull[idx]
    mt = mt.transpose(2, 0, 4, 1, 3).reshape(g, t * p, t * p)

    def pad_n(x, axis):
        pads = [(0, 0)] * x.ndim
        pads[axis] = (0, LANES - n)
        return jnp.pad(x, pads)

    def in_blk(e):
        return pad_n(e.transpose(1, 0, 3, 2).reshape(g, t * p, n), 2)
    win = jnp.concatenate([in_blk(e_re[t - 1::-1, 0][:t]), in_blk(e_im[t - 1::-1, 0][:t]),
                           in_blk(e_re[:t, 1]), in_blk(e_im[:t, 1])], axis=2)
    pf_re, pf_im = pw_re[1:, 0], pw_im[1:, 0]
    pb_re, pb_im = pw_re[t:0:-1, 1], pw_im[t:0:-1, 1]
    def out_blk(c_r, c_i, p_r, p_i):
        ck_re = c_r[None] * p_r[:, :, None, :] - c_i[None] * p_i[:, :, None, :]
        ck_im = c_r[None] * p_i[:, :, None, :] + c_i[None] * p_r[:, :, None, :]
        f = lambda x: pad_n(x.transpose(1, 3, 0, 2).reshape(g, n, t * p), 1)
        return f(ck_re), f(-ck_im)
    wf_re, wf_im = out_blk(c_re[0], c_im[0], pf_re, pf_im)
    wb_re, wb_im = out_blk(c_re[1], c_im[1], pb_re, pb_im)
    wout = jnp.concatenate([wf_re, wf_im, wb_re, wb_im], axis=1)
    apow = jnp.stack([pw_re[t, 0], pw_im[t, 0], pw_re[t, 1], pw_im[t, 1]], axis=1)
    apow = pad_n(apow, 2)
    return mt.astype(BF16), win.astype(BF16), wout.astype(BF16), apow


def _half_masks():
    lane = lax.broadcasted_iota(jnp.int32, (1, LANES), 1)
    lo = lane < HALF
    return lo, jnp.logical_not(lo)


def _swa_kernel(sink_ref, q_ref, kv_ref, bias_ref, o_ref, *, nb_total, nb_step):
    i = pl.program_id(1)
    halves = _half_masks()
    col = lax.broadcasted_iota(jnp.int32, (1, 3 * SWA_BLK), 1)
    for j in range(nb_step):
        gi = i * nb_step + j
        ps = pl.multiple_of(jnp.maximum(gi - 1, 0) * SWA_BLK, SWA_BLK)
        cs = pl.multiple_of(gi * SWA_BLK, SWA_BLK)
        ns = pl.multiple_of(jnp.minimum(gi + 1, nb_total - 1) * SWA_BLK, SWA_BLK)
        kv = jnp.concatenate([kv_ref[pl.ds(ps, SWA_BLK), :], kv_ref[pl.ds(cs, SWA_BLK), :],
                              kv_ref[pl.ds(ns, SWA_BLK), :]], axis=0)
        k2, k2r = kv[:, 0:128], kv[:, 128:256]
        v2, v2r = kv[:, 256:384], kv[:, 384:512]
        lo_col = jnp.where(gi == 0, SWA_BLK, 0)
        hi_col = jnp.where(gi == nb_total - 1, 2 * SWA_BLK, 3 * SWA_BLK)
        valid = jnp.logical_and(col >= lo_col, col < hi_col)
        q2 = q_ref[j * SWA_BLK:(j + 1) * SWA_BLK, :]
        for t in range(2):
            qt = q2[:, t * LANES:(t + 1) * LANES]
            acc = None
            for half in range(2):
                h = 2 * t + half
                sink = sink_ref[h]
                qh = jnp.where(halves[half], qt, jnp.zeros_like(qt))
                kk = k2 if t == half else k2r
                vv = v2 if t == half else v2r
                vv = jnp.where(halves[half], vv, jnp.zeros_like(vv))
                s = _dot_nt(qh, kk) + bias_ref[h]
                s = jnp.where(valid, s, NEG)
                m = jnp.maximum(jnp.max(s, axis=-1, keepdims=True), sink)
                p = jnp.exp(s - m)
                den = jnp.sum(p, axis=-1, keepdims=True) + jnp.exp(sink - m)
                o = _dot(p.astype(BF16), vv) * (1.0 / den)
                acc = o if acc is None else acc + o
            o_ref[j * SWA_BLK:(j + 1) * SWA_BLK, t * LANES:(t + 1) * LANES] = acc.astype(BF16)


def _swa(swa, sink, bias, bsz, seq):
    tq = TOKEN_TILE
    nb_step = tq // SWA_BLK
    return pl.pallas_call(
        functools.partial(_swa_kernel, nb_total=seq // SWA_BLK, nb_step=nb_step),
        grid=(bsz, seq // tq),
        in_specs=[pl.BlockSpec(memory_space=pltpu.SMEM),
                  pl.BlockSpec((None, tq, 256), lambda b, i: (b, i, 2)),
                  pl.BlockSpec((None, seq, 512), lambda b, i: (b, 0, 0)),
                  _const_spec((SWA_HQ, SWA_BLK, 3 * SWA_BLK))],
        out_specs=pl.BlockSpec((None, tq, BRANCH_W), lambda b, i: (b, i, 0)),
        out_shape=jax.ShapeDtypeStruct((bsz, seq, BRANCH_W), BF16),
        compiler_params=_params(("parallel", "parallel")),
        name="swa",
    )(sink, swa, swa, bias)


def _t5_bucket_np(rel):
    nb = T5_BUCKETS // 2
    max_exact = nb // 2
    ret = (rel > 0).astype(np.int32) * nb
    n = np.abs(rel)
    nf = np.maximum(n, 1).astype(np.float32)
    large = max_exact + (np.log(nf / np.float32(max_exact)) / np.float32(math.log(T5_MAX_DIST / max_exact))
                         * np.float32(nb - max_exact)).astype(np.int32)
    large = np.minimum(large, nb - 1)
    return ret + np.where(n < max_exact, n, large)


def _swa_bias(t5_bias):
    qi = np.arange(SWA_BLK)[:, None]
    kj = np.arange(3 * SWA_BLK)[None, :]
    rel = kj - SWA_BLK - qi
    bias = t5_bias[_t5_bucket_np(rel)].transpose(2, 0, 1)
    return jnp.where(jnp.asarray(np.abs(rel) <= SWA_WIN)[None], bias, NEG)


def _na_kernel(q_ref, k_ref, v_ref, bias_ref, o_ref, *, rows, rows_step):
    i = pl.program_id(1)
    halves = _half_masks()
    nkey = NA_KH * GRID_W
    for rr in range(rows_step):
        r = i * rows_step + rr
        rs = jnp.clip(r - NA_KH // 2, 0, rows - NA_KH)
        dr0 = r - rs
        ks = pl.multiple_of(rs * GRID_W, GRID_W)
        kt = k_ref[pl.ds(ks, nkey), :]
        vt = v_ref[pl.ds(ks, nkey), :]
        q2 = q_ref[rr * GRID_W:(rr + 1) * GRID_W, :]
        for t in range(2):
            qt = q2[:, t * LANES:(t + 1) * LANES]
            kk = kt[:, t * LANES:(t + 1) * LANES]
            vv = vt[:, t * LANES:(t + 1) * LANES]
            acc = None
            for half in range(2):
                h = 2 * t + half
                qh = jnp.where(halves[half], qt, jnp.zeros_like(qt))
                vh = jnp.where(halves[half], vv, jnp.zeros_like(vv))
                s = _dot_nt(qh, kk) + bias_ref[dr0, h]
                m = jnp.max(s, axis=-1, keepdims=True)
                p = jnp.exp(s - m)
                den = jnp.sum(p, axis=-1, keepdims=True)
                o = _dot(p.astype(BF16), vh) * (1.0 / den)
                acc = o if acc is None else acc + o
            o_ref[rr * GRID_W:(rr + 1) * GRID_W, t * LANES:(t + 1) * LANES] = acc.astype(BF16)


def _na(na, bias, bsz, seq):
    rows = seq // GRID_W
    assert rows >= NA_KH
    rstep = NA_ROWS_PER_STEP
    tq = rstep * GRID_W
    return pl.pallas_call(
        functools.partial(_na_kernel, rows=rows, rows_step=rstep),
        grid=(bsz, rows // rstep),
        in_specs=[pl.BlockSpec((None, tq, 256), lambda b, i: (b, i, 0)),
                  pl.BlockSpec((None, seq, 256), lambda b, i: (b, 0, 1)),
                  pl.BlockSpec((None, seq, 256), lambda b, i: (b, 0, 2)),
                  _const_spec(bias.shape)],
        out_specs=pl.BlockSpec((None, tq, BRANCH_W), lambda b, i: (b, i, 0)),
        out_shape=jax.ShapeDtypeStruct((bsz, seq, BRANCH_W), BF16),
        compiler_params=_params(("parallel", "parallel")),
        name="na",
    )(na, na, na, bias)


def _na_bias(rpb):
    cols = np.arange(GRID_W)
    cs = np.clip(cols - NA_KW // 2, 0, GRID_W - NA_KW)
    kc = np.arange(GRID_W)
    valid = (kc[None, :] >= cs[:, None]) & (kc[None, :] < cs[:, None] + NA_KW)
    dc = np.clip(kc[None, :] - cols[:, None] + (NA_KW - 1), 0, 2 * NA_KW - 2)
    dr = np.arange(NA_KH)[None, :] - np.arange(NA_KH)[:, None] + (NA_KH - 1)
    tab = rpb[:, dr[:, None, :, None], dc[None, :, None, :]]
    tab = jnp.where(jnp.asarray(valid)[None, None, :, None, :], tab, NEG)
    return tab.transpose(1, 0, 2, 3, 4).reshape(NA_KH, NA_H, GRID_W, NA_KH * GRID_W)


def _mla_kernel(q_ref, k_ref, v_ref, o_ref, *, seq, tk):
    halves = _half_masks()
    tq = q_ref.shape[0]
    for t in range(2):
        out_t = None
        for half in range(2):
            h = 2 * t + half
            qh = q_ref[:, h * LANES:(h + 1) * LANES]

            def body(j, carry, h=h, t=t, half=half, qh=qh):
                m, l, acc = carry
                ks = pl.multiple_of(j * tk, tk)
                kk = k_ref[pl.ds(ks, tk), h * LANES:(h + 1) * LANES]
                vv = v_ref[pl.ds(ks, tk), t * LANES:(t + 1) * LANES]
                vv = jnp.where(halves[half], vv, jnp.zeros_like(vv))
                s = _dot_nt(qh, kk)
                m_new = jnp.maximum(m, jnp.max(s, axis=-1, keepdims=True))
                alpha = jnp.exp2(m - m_new)
                p = jnp.exp2(s - m_new)
                l = alpha * l + jnp.sum(p, axis=-1, keepdims=True)
                acc = alpha * acc + _dot(p.astype(BF16), vv)
                return m_new, l, acc

            init = (jnp.full((tq, 1), -jnp.inf, F32), jnp.zeros((tq, 1), F32),
                    jnp.zeros((tq, LANES), F32))
            _, l, acc = lax.fori_loop(0, seq // tk, body, init)
            o = acc * (1.0 / l)
            out_t = o if out_t is None else out_t + o
        o_ref[:, t * LANES:(t + 1) * LANES] = out_t.astype(BF16)


def _mla(q, k, v, bsz, seq):
    tq = MLA_TQ
    return pl.pallas_call(
        functools.partial(_mla_kernel, seq=seq, tk=MLA_TK),
        grid=(bsz, seq // tq),
        in_specs=[pl.BlockSpec((None, tq, MLA_H * LANES), lambda b, i: (b, i, 0)),
                  pl.BlockSpec((None, seq, MLA_H * LANES), lambda b, i: (b, 0, 0)),
                  pl.BlockSpec((None, seq, BRANCH_W), lambda b, i: (b, 0, 0))],
        out_specs=pl.BlockSpec((None, tq, BRANCH_W), lambda b, i: (b, i, 0)),
        out_shape=jax.ShapeDtypeStruct((bsz, seq, BRANCH_W), BF16),
        compiler_params=_params(("parallel", "parallel")),
        name="mla",
    )(q, k, v)


def _memkv_kernel(m_ref, g_ref, w_ref, o_ref):
    o_ref[...] = _dot(_rms(m_ref[...], g_ref[...]).astype(BF16), w_ref[...]).astype(BF16)


def _memkv(mem, g, w):
    bsz, nm, _ = mem.shape
    wd = w.shape[1]
    return pl.pallas_call(
        _memkv_kernel,
        grid=(bsz,),
        in_specs=[pl.BlockSpec((None, nm, D_MODEL), lambda b: (b, 0, 0)),
                  _const_spec((1, D_MODEL)), _const_spec(w.shape)],
        out_specs=pl.BlockSpec((None, nm, wd), lambda b: (b, 0, 0)),
        out_shape=jax.ShapeDtypeStruct((bsz, nm, wd), BF16),
        compiler_params=_params(("parallel",)),
        name="memkv",
    )(mem, g, w)


def _memattn_kernel(q_ref, kv_ref, o_ref):
    halves = _half_masks()
    hd = MEM_H * HEAD_DIM
    for t in range(2):
        qt = q_ref[:, t * LANES:(t + 1) * LANES]
        kk = kv_ref[:, t * LANES:(t + 1) * LANES]
        vv = kv_ref[:, hd + t * LANES:hd + (t + 1) * LANES]
        acc = None
        for half in range(2):
            qh = jnp.where(halves[half], qt, jnp.zeros_like(qt))
            vh = jnp.where(halves[half], vv, jnp.zeros_like(vv))
            s = _dot_nt(qh, kk)
            m = jnp.max(s, axis=-1, keepdims=True)
            p = jnp.exp(s - m)
            den = jnp.sum(p, axis=-1, keepdims=True)
            o = _dot(p.astype(BF16), vh) * (1.0 / den)
            acc = o if acc is None else acc + o
        o_ref[:, t * LANES:(t + 1) * LANES] = acc.astype(BF16)


def _memattn(q, kv, bsz, seq):
    tq = TOKEN_TILE
    nm, wd = kv.shape[1], kv.shape[2]
    return pl.pallas_call(
        _memattn_kernel,
        grid=(bsz, seq // tq),
        in_specs=[pl.BlockSpec((None, tq, BRANCH_W), lambda b, i: (b, i, 0)),
                  pl.BlockSpec((None, nm, wd), lambda b, i: (b, 0, 0))],
        out_specs=pl.BlockSpec((None, tq, BRANCH_W), lambda b, i: (b, i, 0)),
        out_shape=jax.ShapeDtypeStruct((bsz, seq, BRANCH_W), BF16),
        compiler_params=_params(("parallel", "parallel")),
        name="memattn",
    )(q, kv)


def _merge_kernel(h_ref, g_ref, b0, b1, b2, b3, b4, wg_ref, wb_ref, wo_ref, o_ref):
    h = h_ref[...]
    u = _rms(h, g_ref[...]).astype(BF16)
    merged = None
    for n, br in enumerate((b0, b1, b2, b3, b4)):
        gate = _sigmoid(_dot(u, wg_ref[:, n * D_MODEL:(n + 1) * D_MODEL]))
        term = gate * _dot(br[...], wb_ref[n])
        merged = term if merged is None else merged + term
    o_ref[...] = h + _dot(merged.astype(BF16), wo_ref[...])


def _merge(h, g, branches, wgate, wbr, wout):
    n = h.shape[0]
    tm = TOKEN_TILE
    row = lambda i: (i, 0)
    return pl.pallas_call(
        _merge_kernel,
        grid=(n // tm,),
        in_specs=[pl.BlockSpec((tm, D_MODEL), row), _const_spec((1, D_MODEL))]
                 + [pl.BlockSpec((tm, BRANCH_W), row)] * N_BRANCH
                 + [_const_spec(wgate.shape), _const_spec(wbr.shape), _const_spec(wout.shape)],
        out_specs=pl.BlockSpec((tm, D_MODEL), row),
        out_shape=jax.ShapeDtypeStruct((n, D_MODEL), F32),
        compiler_params=_params(("parallel",)),
        name="merge",
    )(h, g, *branches, wgate, wbr, wout)


def _pack_w_in(w):
    offs = np.cumsum((256, 256, 128, 128, 256, 256, 256, MLA_Q_RANK, MLA_KV_RANK, MLA_ROPE, 256))
    a_in, swa_q, swa_k, swa_v, na_q, na_k, na_v, c_q, c_kv, k_rope, mem_q, gates = jnp.split(w, offs, axis=1)
    sc = HEAD_DIM ** -0.5
    swap = lambda x: jnp.concatenate([x[:, HEAD_DIM:], x[:, :HEAD_DIM]], axis=1)
    z = lambda c: jnp.zeros((w.shape[0], c), w.dtype)
    hr = MLA_ROPE // 2
    x1, x2 = k_rope[:, :hr], k_rope[:, hr:]
    kr_a = jnp.concatenate([z(MLA_NOPE), x1, x2, z(LANES - MLA_NOPE - MLA_ROPE)], axis=1)
    kr_b = jnp.concatenate([z(MLA_NOPE), -x2, x1, z(LANES - MLA_NOPE - MLA_ROPE)], axis=1)
    packed = jnp.concatenate([
        a_in,
        swa_k, swap(swa_k), swa_v, swap(swa_v), swa_q * sc,
        na_q * sc, na_k, na_v,
        mem_q * sc,
        c_q, z(256 - MLA_Q_RANK), c_kv, kr_a, kr_b], axis=1)
    assert packed.shape[1] == IN_COLS
    return packed.astype(BF16), gates.astype(BF16)


def _pack_mla(w_q_up, w_kv_up):
    dq = MLA_NOPE + MLA_ROPE
    hr = MLA_ROPE // 2
    zq = lambda c: jnp.zeros((MLA_Q_RANK, c), w_q_up.dtype)
    qa, qb = [], []
    for h in range(MLA_H):
        blk = w_q_up[:, h * dq:(h + 1) * dq]
        nope, x1, x2 = blk[:, :MLA_NOPE], blk[:, MLA_NOPE:MLA_NOPE + hr], blk[:, MLA_NOPE + hr:]
        qa.append(jnp.concatenate([nope, x1, x2, zq(LANES - dq)], axis=1))
        qb.append(jnp.concatenate([zq(MLA_NOPE), -x2, x1, zq(LANES - dq)], axis=1))
    wq = jnp.concatenate(qa + qb, axis=1)
    wq = jnp.pad(wq, ((0, 256 - MLA_Q_RANK), (0, 0)))
    zk = jnp.zeros((MLA_KV_RANK, LANES - MLA_NOPE), w_kv_up.dtype)
    ks, vs = [], []
    for h in range(MLA_H):
        blk = w_kv_up[:, h * (MLA_NOPE + MLA_V):(h + 1) * (MLA_NOPE + MLA_V)]
        ks.append(jnp.concatenate([blk[:, :MLA_NOPE], zk], axis=1))
        vs.append(blk[:, MLA_NOPE:])
    wkv = jnp.concatenate(ks + vs, axis=1)
    return wq.astype(BF16), wkv.astype(BF16)


def _rope_tables(seq):
    half = MLA_ROPE // 2
    inv = ROPE_THETA ** (-jnp.arange(half, dtype=F32) / half)
    ang = jnp.arange(seq, dtype=F32)[:, None] * inv[None, :]
    cos, sin = jnp.cos(ang), jnp.sin(ang)
    one = jnp.ones((seq, MLA_NOPE), F32)
    z0 = jnp.zeros((seq, MLA_NOPE), F32)
    zt = jnp.zeros((seq, LANES - MLA_NOPE - MLA_ROPE), F32)
    return (jnp.concatenate([one, cos, cos, zt], axis=1),
            jnp.concatenate([z0, sin, sin, zt], axis=1))


def kernel(x_prompt, x_sample, mem_prompt, mem_sample,
           ffn1_norm, ffn1_w_gate, ffn1_w_up, ffn1_w_down,
           mix_norm, w_in,
           ssm_lam_re, ssm_lam_im, ssm_log_step, ssm_b_re, ssm_b_im, ssm_c_re, ssm_c_im,
           ssm_d, ssm_w_glu,
           swa_sink, t5_bias, na_rpb,
           mla_q_norm, mla_w_q_up, mla_kv_norm, mla_w_kv_up,
           mem_norm, mem_w_kv, w_branch, w_out,
           ffn2_norm, ffn2_w_gate, ffn2_w_up, ffn2_w_down,
           final_norm):
    groups = []
    for x, mem in ((x_prompt, mem_prompt), (x_sample, mem_sample)):
        bsz, seq, _ = x.shape
        assert seq % TOKEN_TILE == 0 and seq % MLA_TK == 0 and seq % (NA_ROWS_PER_STEP * GRID_W) == 0
        groups.append(dict(b=bsz, s=seq, h=x.reshape(bsz * seq, D_MODEL), mem=mem))
    max_seq = max(g['s'] for g in groups)
    cos_t, sin_t = _rope_tables(max_seq)
    swa_bias = _swa_bias(t5_bias.astype(F32))
    row = lambda v: v.reshape(1, -1)

    segments, base = [], 0
    for g in groups:
        for _ in range(g['b']):
            segments.append((base, g['s'] // SSM_T))
            base += g['s'] // SSM_T
    segments = tuple(segments)

    for l in range(DEPTH):
        w_pack, w_gate = _pack_w_in(w_in[l])
        wq, wkv = _pack_mla(mla_w_q_up[l], mla_w_kv_up[l])
        qn = row(jnp.pad(mla_q_norm[l], (0, 256 - MLA_Q_RANK)))
        na_bias = _na_bias(na_rpb[l].astype(F32))
        mt, win, wout, apow = _ssm_tables(ssm_lam_re[l], ssm_lam_im[l], ssm_log_step[l],
                                          ssm_b_re[l], ssm_b_im[l], ssm_c_re[l], ssm_c_im[l])
        f1 = (row(ffn1_norm[l]), ffn1_w_gate[l].astype(BF16), ffn1_w_up[l].astype(BF16),
              ffn1_w_down[l].astype(BF16))
        f2 = (row(ffn2_norm[l]), ffn2_w_gate[l].astype(BF16), ffn2_w_up[l].astype(BF16),
              ffn2_w_down[l].astype(BF16))
        wbr = w_branch[l].astype(BF16)
        wo = w_out[l].astype(BF16)
        wmem = mem_w_kv[l].astype(BF16)
        wglu = ssm_w_glu[l].astype(BF16)

        for g in groups:
            g['h'] = _ffn(g['h'], *f1)
            (g['a'], g['swa'], g['na'], g['mq'], g['q'], g['k'], g['v']) = _inproj(
                g['h'], g['s'], row(mix_norm[l]), w_pack, wq, wkv, qn, row(mla_kv_norm[l]),
                cos_t, sin_t)

        a_all = jnp.concatenate([g['a'] for g in groups], axis=0)
        n_all = a_all.shape[0]
        u_g = (a_all.astype(BF16).reshape(n_all // SSM_T, SSM_T, SSM_G, SSM_P)
               .transpose(2, 0, 1, 3).reshape(SSM_G, n_all // SSM_T, SSM_TP))
        y_g = _ssm_conv(u_g, mt, win, wout, apow, segments)
        y_all = (y_g.reshape(SSM_G, n_all // SSM_T, SSM_T, SSM_P)
                 .transpose(1, 2, 0, 3).reshape(n_all, BRANCH_W))
        br0_all = _ssm_post(a_all, y_all, row(ssm_d[l]), wglu)

        off = 0
        for g in groups:
            bsz, seq = g['b'], g['s']
            n = bsz * seq
            r3 = lambda v: v.reshape(bsz, seq, v.shape[-1])
            br0 = br0_all[off:off + n]
            off += n
            br1 = _swa(r3(g['swa']), swa_sink[l].astype(F32), swa_bias, bsz, seq)
            br2 = _na(r3(g['na']), na_bias, bsz, seq)
            br3 = _mla(r3(g['q']), r3(g['k']), r3(g['v']), bsz, seq)
            kvm = _memkv(g['mem'], row(mem_norm[l]), wmem)
            br4 = _memattn(r3(g['mq']), kvm, bsz, seq)
            flat = lambda v: v.reshape(n, BRANCH_W)
            g['h'] = _merge(g['h'], row(mix_norm[l]),
                            (br0, flat(br1), flat(br2), flat(br3), flat(br4)),
                            w_gate, wbr, wo)
            g['h'] = _ffn(g['h'], *f2, final_g=row(final_norm) if l == DEPTH - 1 else None)

    return tuple(g['h'].reshape(g['b'], g['s'], D_MODEL) for g in groups)
```

```python
import functools
import math

import numpy as np
import jax
import jax.numpy as jnp
from jax import lax
from jax.experimental import pallas as pl
from jax.experimental.pallas import tpu as pltpu

F32 = jnp.float32
BF16 = jnp.bfloat16

D_MODEL = 1024
DEPTH = 2
GRID_W = 64
D_FF = 2816
HEAD_DIM = 64
BRANCH_W = 256
N_BRANCH = 5
SSM_P = 16
SSM_G = BRANCH_W // SSM_P
SSM_N = 64
SWA_HQ = 4
SWA_HKV = 2
SWA_WIN = 128
SWA_BLK = 128
T5_BUCKETS = 32
T5_MAX_DIST = 128
NA_H = 4
NA_KH = 8
NA_KW = 16
MLA_H = 4
MLA_Q_RANK = 192
MLA_KV_RANK = 128
MLA_NOPE = 64
MLA_ROPE = 32
MLA_V = 64
ROPE_THETA = 10000.0
MEM_H = 4
EPS = 1e-6
NEG = -1e30

LANES = 128
SUBLANES = 8
HALF = LANES // 2
SSM_T = LANES
SSM_TP = SSM_T * SSM_P
SSM_SW = 4 * LANES
FFN_CHUNK = 256
TOKEN_TILE = 512
MLA_TQ = 512
MLA_TK = 2048
NA_ROWS_PER_STEP = 8
VMEM_LIMIT = 56 * 1024 * 1024
MLA_QSCALE = (MLA_NOPE + MLA_ROPE) ** -0.5 * math.log2(math.e)

IN_SWA = (0, 768)
IN_NA = (768, 1536)
IN_MEMQ = (1536, 1792)
IN_LAT = (1792, 2432)
IN_COLS = 2432


def _dot(a, b):
    return jnp.dot(a, b, preferred_element_type=F32)


def _dot_nt(a, b):
    return lax.dot_general(a, b, (((1,), (1,)), ((), ())), preferred_element_type=F32)


def _rms(x, g):
    return x * lax.rsqrt(jnp.mean(x * x, axis=-1, keepdims=True) + EPS) * g


def _sigmoid(x):
    return 1.0 / (1.0 + jnp.exp(-x))


def _params(sem):
    return pltpu.CompilerParams(dimension_semantics=sem, vmem_limit_bytes=VMEM_LIMIT)


def _const_spec(shape):
    nd = len(shape)
    return pl.BlockSpec(shape, lambda *_: (0,) * nd, pipeline_mode=pl.Buffered(1))


def _lo_lanes():
    return lax.broadcasted_iota(jnp.int32, (1, LANES), 1) < HALF


def _stack_heads(qt):
    lo = _lo_lanes()
    z = jnp.zeros_like(qt)
    return jnp.concatenate([jnp.where(lo, qt, z), jnp.where(lo, z, qt)], axis=0)


def _unstack_heads(o):
    n = o.shape[0] // 2
    return jnp.where(_lo_lanes(), o[:n], o[n:])


def _ffn_kernel(*refs, final):
    if final:
        x_ref, g_ref, wg_ref, wu_ref, wd_ref, fn_ref, o_ref = refs
    else:
        x_ref, g_ref, wg_ref, wu_ref, wd_ref, o_ref = refs
    x = x_ref[...]
    xn = _rms(x, g_ref[...]).astype(BF16)
    acc = jnp.zeros(x.shape, F32)
    for c in range(D_FF // FFN_CHUNK):
        sl = slice(c * FFN_CHUNK, (c + 1) * FFN_CHUNK)
        g = _dot(xn, wg_ref[:, sl])
        u = _dot(xn, wu_ref[:, sl])
        a = (g * _sigmoid(g) * u).astype(BF16)
        acc = acc + _dot(a, wd_ref[sl, :])
    y = x + 0.5 * acc
    if final:
        y = _rms(y, fn_ref[...])
    o_ref[...] = y


def _ffn(x, g, wg, wu, wd, final_g=None):
    n = x.shape[0]
    tm = TOKEN_TILE
    final = final_g is not None
    ins = [x, g, wg, wu, wd]
    specs = [pl.BlockSpec((tm, D_MODEL), lambda i: (i, 0)),
             _const_spec((1, D_MODEL)),
             _const_spec((D_MODEL, D_FF)), _const_spec((D_MODEL, D_FF)),
             _const_spec((D_FF, D_MODEL))]
    if final:
        ins.append(final_g)
        specs.append(_const_spec((1, D_MODEL)))
    return pl.pallas_call(
        functools.partial(_ffn_kernel, final=final),
        grid=(n // tm,),
        in_specs=specs,
        out_specs=pl.BlockSpec((tm, D_MODEL), lambda i: (i, 0)),
        out_shape=jax.ShapeDtypeStruct((n, D_MODEL), F32),
        compiler_params=_params(("parallel",)),
        name="ffn_final" if final else "ffn",
    )(*ins)


def _inproj_kernel(x_ref, g_ref, wa_ref, w_ref, wq_ref, wkv_ref, qn_ref, kvn_ref, cos_ref, sin_ref,
                   a_ref, swa_ref, na_ref, mq_ref, q_ref, k_ref, v_ref):
    u = _rms(x_ref[...], g_ref[...]).astype(BF16)
    a_ref[...] = _dot_nt(wa_ref[...], u)
    swa_ref[...] = _dot(u, w_ref[:, IN_SWA[0]:IN_SWA[1]]).astype(BF16)
    na_ref[...] = _dot(u, w_ref[:, IN_NA[0]:IN_NA[1]]).astype(BF16)
    mq_ref[...] = _dot(u, w_ref[:, IN_MEMQ[0]:IN_MEMQ[1]]).astype(BF16)
    lat = _dot(u, w_ref[:, IN_LAT[0]:IN_LAT[1]])
    cq = lat[:, 0:256]
    ckv = lat[:, 256:384]
    kra = lat[:, 384:512]
    krb = lat[:, 512:640]
    cos = cos_ref[...]
    sin = sin_ref[...]
    ms = jnp.sum(cq * cq, axis=-1, keepdims=True) * (1.0 / MLA_Q_RANK)
    cqn = (cq * lax.rsqrt(ms + EPS) * qn_ref[...]).astype(BF16)
    qab = _dot(cqn, wq_ref[...])
    cos4 = jnp.concatenate([cos] * MLA_H, axis=1)
    sin4 = jnp.concatenate([sin] * MLA_H, axis=1)
    nq = MLA_H * LANES
    q = (qab[:, :nq] * cos4 + qab[:, nq:] * sin4) * MLA_QSCALE
    q_ref[...] = q.astype(BF16)
    ckvn = _rms(ckv, kvn_ref[...]).astype(BF16)
    kv = _dot(ckvn, wkv_ref[...])
    kr = kra * cos + krb * sin
    k_ref[...] = (kv[:, :nq] + jnp.concatenate([kr] * MLA_H, axis=1)).astype(BF16)
    v_ref[...] = kv[:, nq:].astype(BF16)


def _inproj(x, seq_len, g, wa_t, w, wq, wkv, qn, kvn, cos_t, sin_t):
    n = x.shape[0]
    tm = TOKEN_TILE
    ns = seq_len // tm
    widths = (768, 768, 256, 512, 512, 256)
    row = lambda i: (i, 0)
    return pl.pallas_call(
        _inproj_kernel,
        grid=(n // tm,),
        in_specs=[pl.BlockSpec((tm, D_MODEL), row),
                  _const_spec((1, D_MODEL)),
                  _const_spec((BRANCH_W, D_MODEL)),
                  _const_spec((D_MODEL, IN_COLS)),
                  _const_spec(wq.shape), _const_spec(wkv.shape),
                  _const_spec((1, 256)), _const_spec((1, MLA_KV_RANK)),
                  pl.BlockSpec((tm, LANES), lambda i: (i % ns, 0)),
                  pl.BlockSpec((tm, LANES), lambda i: (i % ns, 0))],
        out_specs=[pl.BlockSpec((BRANCH_W, tm), lambda i: (0, i))]
                  + [pl.BlockSpec((tm, wd), row) for wd in widths],
        out_shape=[jax.ShapeDtypeStruct((BRANCH_W, n), F32)]
                  + [jax.ShapeDtypeStruct((n, wd), BF16) for wd in widths],
        compiler_params=_params(("parallel",)),
        name="inproj",
    )(x, g, wa_t, w, wq, wkv, qn, kvn, cos_t, sin_t)


def _toeplitz_kernel(v_ref, o_ref):
    rows = o_ref.shape[0]
    n = v_ref.shape[0]
    w = v_ref.shape[1]
    for p in range(n):
        x = jnp.broadcast_to(v_ref[p:p + 1, :], (rows, w))
        r = pltpu.roll(x, 1, 1, stride=1, stride_axis=0)
        o_ref[:, p * (w - rows):(p + 1) * (w - rows)] = r[:, rows:].astype(o_ref.dtype)


def _ssm_toeplitz(vtab):
    g, q, p, w = vtab.shape
    t = w // 2
    return pl.pallas_call(
        _toeplitz_kernel,
        grid=(g, q),
        in_specs=[pl.BlockSpec((None, None, p, w), lambda i, j: (i, j, 0, 0))],
        out_specs=pl.BlockSpec((None, t, p * t), lambda i, j: (i, j, 0)),
        out_shape=jax.ShapeDtypeStruct((g, q * t, p * t), BF16),
        compiler_params=_params(("parallel", "parallel")),
        name="ssm_toeplitz",
    )(vtab)


def _ssm_conv_kernel(*refs, segments, n_in):
    u_refs = refs[:n_in]
    mt_ref, win_ref, wout_ref, ap_ref = refs[n_in:n_in + 4]
    y_refs = refs[n_in + 4:2 * n_in + 4]
    s_scr, x_scr = refs[2 * n_in + 4:]
    u = jnp.concatenate(
        [jnp.concatenate([r[q] for q in range(SSM_P)], axis=1) for r in u_refs], axis=0).astype(BF16)
    s_scr[...] = _dot(u, win_ref[...])
    afr = ap_ref[0:1, :]
    afi = ap_ref[1:2, :]
    abr = ap_ref[2:3, :]
    abi = ap_ref[3:4, :]
    z = jnp.zeros((1, LANES), F32)
    sub = SUBLANES
    for base, nch in segments:
        def body(c, carry, base=base, nch=nch):
            xfr, xfi, xbr, xbi = carry
            rf = pl.multiple_of(base + c * sub, sub)
            rb = pl.multiple_of(base + nch - (c + 1) * sub, sub)
            sf = s_scr[pl.ds(rf, sub), 0:2 * LANES]
            sb = s_scr[pl.ds(rb, sub), 2 * LANES:4 * LANES]
            ofr, ofi, obr, obi = [], [], [None] * sub, [None] * sub
            for k in range(sub):
                ofr.append(xfr)
                ofi.append(xfi)
                xfr, xfi = (afr * xfr - afi * xfi + sf[k:k + 1, 0:LANES],
                            afr * xfi + afi * xfr + sf[k:k + 1, LANES:2 * LANES])
                kb = sub - 1 - k
                obr[kb] = xbr
                obi[kb] = xbi
                xbr, xbi = (abr * xbr - abi * xbi + sb[kb:kb + 1, 0:LANES],
                            abr * xbi + abi * xbr + sb[kb:kb + 1, LANES:2 * LANES])
            x_scr[pl.ds(rf, sub), 0:LANES] = jnp.concatenate(ofr, axis=0)
            x_scr[pl.ds(rf, sub), LANES:2 * LANES] = jnp.concatenate(ofi, axis=0)
            x_scr[pl.ds(rb, sub), 2 * LANES:3 * LANES] = jnp.concatenate(obr, axis=0)
            x_scr[pl.ds(rb, sub), 3 * LANES:4 * LANES] = jnp.concatenate(obi, axis=0)
            return xfr, xfi, xbr, xbi
        assert base % sub == 0 and nch % sub == 0
        lax.fori_loop(0, nch // sub, body, (z, z, z, z))
    y = _dot(u, mt_ref[...]) + _dot(x_scr[...].astype(BF16), wout_ref[...])
    row = 0
    for r in y_refs:
        nr = r.shape[1]
        for p in range(SSM_P):
            r[p] = y[row:row + nr, p * SSM_T:(p + 1) * SSM_T]
        row += nr


def _ssm_conv(u_list, mt, win, wout, apow, segments):
    nc = sum(u.shape[2] for u in u_list)
    blk = lambda u: pl.BlockSpec((None, SSM_P, u.shape[2], SSM_T), lambda i: (i, 0, 0, 0))
    w3 = lambda a: pl.BlockSpec((None,) + a.shape[1:], lambda i: (i, 0, 0))
    return pl.pallas_call(
        functools.partial(_ssm_conv_kernel, segments=segments, n_in=len(u_list)),
        grid=(SSM_G,),
        in_specs=[blk(u) for u in u_list] + [w3(mt), w3(win), w3(wout), w3(apow)],
        out_specs=[blk(u) for u in u_list],
        out_shape=[jax.ShapeDtypeStruct(u.shape, F32) for u in u_list],
        scratch_shapes=[pltpu.VMEM((nc, SSM_SW), F32), pltpu.VMEM((nc, SSM_SW), F32)],
        compiler_params=_params(("parallel",)),
        name="ssm_conv",
    )(*u_list, mt, win, wout, apow)


def _ssm_post_kernel(a_ref, y_ref, d_ref, w_ref, o_ref):
    y = d_ref[...] * a_ref[...] + y_ref[...]
    c = math.sqrt(2.0 / math.pi)
    gl = y * (0.5 * (1.0 + jnp.tanh(c * (y + 0.044715 * (y * y * y)))))
    o = gl * _sigmoid(_dot(w_ref[...], gl.astype(BF16)))
    o_ref[...] = o.T.astype(BF16)


def _ssm_post(a_t, y_t, d, w_t):
    n = a_t.shape[1]
    tm = TOKEN_TILE
    col = lambda i: (0, i)
    return pl.pallas_call(
        _ssm_post_kernel,
        grid=(n // tm,),
        in_specs=[pl.BlockSpec((BRANCH_W, tm), col), pl.BlockSpec((BRANCH_W, tm), col),
                  _const_spec((BRANCH_W, 1)), _const_spec((BRANCH_W, BRANCH_W))],
        out_specs=pl.BlockSpec((tm, BRANCH_W), lambda i: (i, 0)),
        out_shape=jax.ShapeDtypeStruct((n, BRANCH_W), BF16),
        compiler_params=_params(("parallel",)),
        name="ssm_post",
    )(a_t, y_t, d, w_t)


def _ssm_tables(lam_re, lam_im, log_step, b_re, b_im, c_re, c_im):
    hp = lax.Precision.HIGHEST
    t, g, n, p = SSM_T, SSM_G, SSM_N, SSM_P
    dt = jnp.exp(log_step)[..., None]
    lr, li = lam_re, lam_im
    mag = jnp.exp(lr * dt)
    a_re = mag * jnp.cos(li * dt)
    a_im = mag * jnp.sin(li * dt)
    den = lr * lr + li * li
    xr = a_re - 1.0
    k_re = (xr * lr + a_im * li) / den
    k_im = (a_im * lr - xr * li) / den
    bb_re = k_re[..., None] * b_re - k_im[..., None] * b_im
    bb_im = k_re[..., None] * b_im + k_im[..., None] * b_re
    kk = jnp.arange(t + 1, dtype=F32)[:, None, None, None]
    pmag = jnp.exp(kk * (lr * dt)[None])
    ang = kk * (li * dt)[None]
    pw_re = pmag * jnp.cos(ang)
    pw_im = pmag * jnp.sin(ang)
    e_re = pw_re[..., None] * bb_re[None] - pw_im[..., None] * bb_im[None]
    e_im = pw_re[..., None] * bb_im[None] + pw_im[..., None] * bb_re[None]
    kd = (jnp.einsum('dgpn,tdgnq->dtgpq', c_re, e_re[:t], precision=hp)
          - jnp.einsum('dgpn,tdgnq->dtgpq', c_im, e_im[:t], precision=hp))
    kf, kb = kd[0], kd[1]
    kfull = jnp.concatenate([kb[:0:-1], (kf[0] + kb[0])[None], kf[1:]], axis=0)
    vtab = jnp.pad(kfull.transpose(1, 3, 2, 0), ((0, 0), (0, 0), (0, 0), (0, 1)))

    def pad_n(x, axis):
        pads = [(0, 0)] * x.ndim
        pads[axis] = (0, LANES - n)
        return jnp.pad(x, pads)

    def in_blk(e):
        return pad_n(e.transpose(1, 3, 0, 2).reshape(g, p * t, n), 2)
    win = jnp.concatenate([in_blk(e_re[t - 1::-1, 0]), in_blk(e_im[t - 1::-1, 0]),
                           in_blk(e_re[:t, 1]), in_blk(e_im[:t, 1])], axis=2)
    pf_re, pf_im = pw_re[1:, 0], pw_im[1:, 0]
    pb_re, pb_im = pw_re[t:0:-1, 1], pw_im[t:0:-1, 1]

    def out_blk(c_r, c_i, p_r, p_i):
        ck_re = c_r[None] * p_r[:, :, None, :] - c_i[None] * p_i[:, :, None, :]
        ck_im = c_r[None] * p_i[:, :, None, :] + c_i[None] * p_r[:, :, None, :]
        f = lambda x: pad_n(x.transpose(1, 3, 2, 0).reshape(g, n, p * t), 1)
        return f(ck_re), f(-ck_im)
    wf_re, wf_im = out_blk(c_re[0], c_im[0], pf_re, pf_im)
    wb_re, wb_im = out_blk(c_re[1], c_im[1], pb_re, pb_im)
    wout = jnp.concatenate([wf_re, wf_im, wb_re, wb_im], axis=1)
    apow = jnp.stack([pw_re[t, 0], pw_im[t, 0], pw_re[t, 1], pw_im[t, 1]], axis=1)
    apow = pad_n(apow, 2)
    return vtab, win.astype(BF16), wout.astype(BF16), apow


def _swa_kernel(sink_ref, q_ref, kv_ref, bias_ref, o_ref, *, nb_total, nb_step):
    i = pl.program_id(1)
    col = lax.broadcasted_iota(jnp.int32, (1, 3 * SWA_BLK), 1)
    top = lax.broadcasted_iota(jnp.int32, (2 * SWA_BLK, 1), 0) < SWA_BLK
    for j in range(nb_step):
        gi = i * nb_step + j
        ps = pl.multiple_of(jnp.maximum(gi - 1, 0) * SWA_BLK, SWA_BLK)
        cs = pl.multiple_of(gi * SWA_BLK, SWA_BLK)
        ns = pl.multiple_of(jnp.minimum(gi + 1, nb_total - 1) * SWA_BLK, SWA_BLK)
        kv = jnp.concatenate([kv_ref[pl.ds(ps, SWA_BLK), :], kv_ref[pl.ds(cs, SWA_BLK), :],
                              kv_ref[pl.ds(ns, SWA_BLK), :]], axis=0)
        lo_col = jnp.where(gi == 0, SWA_BLK, 0)
        hi_col = jnp.where(gi == nb_total - 1, 2 * SWA_BLK, 3 * SWA_BLK)
        valid = jnp.logical_and(col >= lo_col, col < hi_col)
        q2 = q_ref[j * SWA_BLK:(j + 1) * SWA_BLK, :]
        for t in range(SWA_HKV):
            kk = kv[:, t * LANES:(t + 1) * LANES]
            vv = kv[:, (SWA_HKV + t) * LANES:(SWA_HKV + t + 1) * LANES]
            sink = jnp.where(top, sink_ref[2 * t], sink_ref[2 * t + 1])
            s = _dot_nt(_stack_heads(q2[:, t * LANES:(t + 1) * LANES]), kk) + bias_ref[t]
            s = jnp.where(valid, s, NEG)
            m = jnp.maximum(jnp.max(s, axis=-1, keepdims=True), sink)
            p = jnp.exp(s - m)
            den = jnp.sum(p, axis=-1, keepdims=True) + jnp.exp(sink - m)
            o = _dot(p.astype(BF16), vv) * (1.0 / den)
            o_ref[j * SWA_BLK:(j + 1) * SWA_BLK, t * LANES:(t + 1) * LANES] = _unstack_heads(o).astype(BF16)


def _swa(swa, sink, bias, bsz, seq):
    tq = TOKEN_TILE
    nb_step = tq // SWA_BLK
    return pl.pallas_call(
        functools.partial(_swa_kernel, nb_total=seq // SWA_BLK, nb_step=nb_step),
        grid=(bsz, seq // tq),
        in_specs=[pl.BlockSpec(memory_space=pltpu.SMEM),
                  pl.BlockSpec((None, tq, 256), lambda b, i: (b, i, 2)),
                  pl.BlockSpec((None, seq, 512), lambda b, i: (b, 0, 0)),
                  _const_spec(bias.shape)],
        out_specs=pl.BlockSpec((None, tq, BRANCH_W), lambda b, i: (b, i, 0)),
        out_shape=jax.ShapeDtypeStruct((bsz, seq, BRANCH_W), BF16),
        compiler_params=_params(("parallel", "parallel")),
        name="swa",
    )(sink, swa, swa, bias)


def _t5_bucket_np(rel):
    nb = T5_BUCKETS // 2
    max_exact = nb // 2
    ret = (rel > 0).astype(np.int32) * nb
    n = np.abs(rel)
    nf = np.maximum(n, 1).astype(np.float32)
    large = max_exact + (np.log(nf / np.float32(max_exact)) / np.float32(math.log(T5_MAX_DIST / max_exact))
                         * np.float32(nb - max_exact)).astype(np.int32)
    large = np.minimum(large, nb - 1)
    return ret + np.where(n < max_exact, n, large)


def _t5_expand_kernel(v_ref, o_ref):
    rows, w = o_ref.shape[0], v_ref.shape[1]
    x = jnp.broadcast_to(v_ref[...], (rows, w))
    o_ref[...] = pltpu.roll(x, 1, 1, stride=1, stride_axis=0)[:, rows:]


def _swa_bias(t5_bias):
    rel = np.arange(-(2 * SWA_BLK - 1), 2 * SWA_BLK)
    rel = np.concatenate([rel, [2 * SWA_BLK]])
    onehot = jnp.asarray(np.eye(T5_BUCKETS, dtype=np.float32)[_t5_bucket_np(rel)])
    vec = jnp.dot(onehot, t5_bias, precision=lax.Precision.HIGHEST).T
    vec = jnp.where(jnp.asarray(np.abs(rel) <= SWA_WIN)[None], vec, NEG)
    w = vec.shape[1]
    tab = pl.pallas_call(
        _t5_expand_kernel,
        grid=(SWA_HQ,),
        in_specs=[pl.BlockSpec((None, 1, w), lambda h: (h, 0, 0))],
        out_specs=pl.BlockSpec((None, SWA_BLK, 3 * SWA_BLK), lambda h: (h, 0, 0)),
        out_shape=jax.ShapeDtypeStruct((SWA_HQ, SWA_BLK, 3 * SWA_BLK), F32),
        name="t5_expand",
    )(vec.reshape(SWA_HQ, 1, w))
    return tab.reshape(SWA_HKV, 2 * SWA_BLK, 3 * SWA_BLK)


def _na_kernel(q_ref, k_ref, v_ref, bias_ref, o_ref, *, rows, rows_step):
    i = pl.program_id(1)
    nkey = NA_KH * GRID_W
    for rr in range(rows_step):
        r = i * rows_step + rr
        rs = jnp.clip(r - NA_KH // 2, 0, rows - NA_KH)
        dr0 = r - rs
        ks = pl.multiple_of(rs * GRID_W, GRID_W)
        kt = k_ref[pl.ds(ks, nkey), :]
        vt = v_ref[pl.ds(ks, nkey), :]
        q2 = q_ref[rr * GRID_W:(rr + 1) * GRID_W, :]
        for t in range(NA_H // 2):
            sl = slice(t * LANES, (t + 1) * LANES)
            s = _dot_nt(_stack_heads(q2[:, sl]), kt[:, sl]) + bias_ref[dr0, t]
            m = jnp.max(s, axis=-1, keepdims=True)
            p = jnp.exp(s - m)
            den = jnp.sum(p, axis=-1, keepdims=True)
            o = _dot(p.astype(BF16), vt[:, sl]) * (1.0 / den)
            o_ref[rr * GRID_W:(rr + 1) * GRID_W, sl] = _unstack_heads(o).astype(BF16)


def _na(na, bias, bsz, seq):
    rows = seq // GRID_W
    assert rows >= NA_KH
    rstep = NA_ROWS_PER_STEP
    tq = rstep * GRID_W
    return pl.pallas_call(
        functools.partial(_na_kernel, rows=rows, rows_step=rstep),
        grid=(bsz, rows // rstep),
        in_specs=[pl.BlockSpec((None, tq, 256), lambda b, i: (b, i, 0)),
                  pl.BlockSpec((None, seq, 256), lambda b, i: (b, 0, 1)),
                  pl.BlockSpec((None, seq, 256), lambda b, i: (b, 0, 2)),
                  _const_spec(bias.shape)],
        out_specs=pl.BlockSpec((None, tq, BRANCH_W), lambda b, i: (b, i, 0)),
        out_shape=jax.ShapeDtypeStruct((bsz, seq, BRANCH_W), BF16),
        compiler_params=_params(("parallel", "parallel")),
        name="na",
    )(na, na, na, bias)


def _na_bias(rpb):
    hp = lax.Precision.HIGHEST
    cols = np.arange(GRID_W)
    cs = np.clip(cols - NA_KW // 2, 0, GRID_W - NA_KW)
    kc = np.arange(GRID_W)
    valid = (kc[None, :] >= cs[:, None]) & (kc[None, :] < cs[:, None] + NA_KW)
    dc = np.clip(kc[None, :] - cols[:, None] + (NA_KW - 1), 0, 2 * NA_KW - 2)
    dr = np.arange(NA_KH)[None, :] - np.arange(NA_KH)[:, None] + (NA_KH - 1)
    oh_c = np.eye(2 * NA_KW - 1, dtype=np.float32)[dc] * valid[..., None]
    oh_r = np.eye(2 * NA_KH - 1, dtype=np.float32)[dr]
    tab = jnp.einsum('dka,hab,cjb->dhckj', jnp.asarray(oh_r), rpb, jnp.asarray(oh_c), precision=hp)
    tab = jnp.where(jnp.asarray(valid)[None, None, :, None, :], tab, NEG)
    return tab.reshape(NA_KH, NA_H // 2, 2 * GRID_W, NA_KH * GRID_W)


def _mla_kernel(q_ref, k_ref, v_ref, o_ref, *, seq, tk):
    tq = q_ref.shape[0]

    def body(j, carry):
        ks = pl.multiple_of(j * tk, tk)
        out = []
        for h in range(MLA_H):
            m, l, acc = carry[3 * h:3 * h + 3]
            hs = slice(h * LANES, (h + 1) * LANES)
            ts = slice((h // 2) * LANES, (h // 2 + 1) * LANES)
            s = _dot_nt(q_ref[:, hs], k_ref[pl.ds(ks, tk), hs])
            m_new = jnp.maximum(m, jnp.max(s, axis=-1, keepdims=True))
            alpha = jnp.exp2(m - m_new)
            p = jnp.exp2(s - m_new)
            l = alpha * l + jnp.sum(p, axis=-1, keepdims=True)
            acc = alpha * acc + _dot(p.astype(BF16), v_ref[pl.ds(ks, tk), ts])
            out += [m_new, l, acc]
        return tuple(out)

    init = (jnp.full((tq, 1), -jnp.inf, F32), jnp.zeros((tq, 1), F32), jnp.zeros((tq, LANES), F32)) * MLA_H
    res = lax.fori_loop(0, seq // tk, body, init)
    lo = _lo_lanes()
    for t in range(MLA_H // 2):
        oa = res[6 * t + 2] * (1.0 / res[6 * t + 1])
        ob = res[6 * t + 5] * (1.0 / res[6 * t + 4])
        o_ref[:, t * LANES:(t + 1) * LANES] = jnp.where(lo, oa, ob).astype(BF16)


def _mla(q, k, v, bsz, seq):
    tq = MLA_TQ
    tk = min(MLA_TK, seq)
    assert seq % tk == 0 and seq % tq == 0
    return pl.pallas_call(
        functools.partial(_mla_kernel, seq=seq, tk=tk),
        grid=(bsz, seq // tq),
        in_specs=[pl.BlockSpec((None, tq, MLA_H * LANES), lambda b, i: (b, i, 0)),
                  pl.BlockSpec((None, seq, MLA_H * LANES), lambda b, i: (b, 0, 0)),
                  pl.BlockSpec((None, seq, BRANCH_W), lambda b, i: (b, 0, 0))],
        out_specs=pl.BlockSpec((None, tq, BRANCH_W), lambda b, i: (b, i, 0)),
        out_shape=jax.ShapeDtypeStruct((bsz, seq, BRANCH_W), BF16),
        compiler_params=_params(("parallel", "parallel")),
        name="mla",
    )(q, k, v)


def _memkv_kernel(m_ref, g_ref, w_ref, o_ref):
    o_ref[...] = _dot(_rms(m_ref[...], g_ref[...]).astype(BF16), w_ref[...]).astype(BF16)


def _memkv(mem, g, w):
    bsz, nm, _ = mem.shape
    wd = w.shape[1]
    return pl.pallas_call(
        _memkv_kernel,
        grid=(bsz,),
        in_specs=[pl.BlockSpec((None, nm, D_MODEL), lambda b: (b, 0, 0)),
                  _const_spec((1, D_MODEL)), _const_spec(w.shape)],
        out_specs=pl.BlockSpec((None, nm, wd), lambda b: (b, 0, 0)),
        out_shape=jax.ShapeDtypeStruct((bsz, nm, wd), BF16),
        compiler_params=_params(("parallel",)),
        name="memkv",
    )(mem, g, w)


def _memattn_kernel(q_ref, kv_ref, o_ref):
    hd = MEM_H * HEAD_DIM
    for t in range(MEM_H // 2):
        sl = slice(t * LANES, (t + 1) * LANES)
        s = _dot_nt(_stack_heads(q_ref[:, sl]), kv_ref[:, sl])
        m = jnp.max(s, axis=-1, keepdims=True)
        p = jnp.exp(s - m)
        den = jnp.sum(p, axis=-1, keepdims=True)
        o = _dot(p.astype(BF16), kv_ref[:, hd + t * LANES:hd + (t + 1) * LANES]) * (1.0 / den)
        o_ref[:, sl] = _unstack_heads(o).astype(BF16)


def _memattn(q, kv, bsz, seq):
    tq = TOKEN_TILE
    nm, wd = kv.shape[1], kv.shape[2]
    return pl.pallas_call(
        _memattn_kernel,
        grid=(bsz, seq // tq),
        in_specs=[pl.BlockSpec((None, tq, BRANCH_W), lambda b, i: (b, i, 0)),
                  pl.BlockSpec((None, nm, wd), lambda b, i: (b, 0, 0))],
        out_specs=pl.BlockSpec((None, tq, BRANCH_W), lambda b, i: (b, i, 0)),
        out_shape=jax.ShapeDtypeStruct((bsz, seq, BRANCH_W), BF16),
        compiler_params=_params(("parallel", "parallel")),
        name="memattn",
    )(q, kv)


def _merge_kernel(h_ref, g_ref, b0, b1, b2, b3, b4, wg_ref, wb_ref, wo_ref, o_ref):
    h = h_ref[...]
    u = _rms(h, g_ref[...]).astype(BF16)
    merged = None
    for n, br in enumerate((b0, b1, b2, b3, b4)):
        gate = _sigmoid(_dot(u, wg_ref[:, n * D_MODEL:(n + 1) * D_MODEL]))
        term = gate * _dot(br[...], wb_ref[n])
        merged = term if merged is None else merged + term
    o_ref[...] = h + _dot(merged.astype(BF16), wo_ref[...])


def _merge(h, g, branches, wgate, wbr, wout):
    n = h.shape[0]
    tm = TOKEN_TILE
    row = lambda i: (i, 0)
    return pl.pallas_call(
        _merge_kernel,
        grid=(n // tm,),
        in_specs=[pl.BlockSpec((tm, D_MODEL), row), _const_spec((1, D_MODEL))]
                 + [pl.BlockSpec((tm, BRANCH_W), row)] * N_BRANCH
                 + [_const_spec(wgate.shape), _const_spec(wbr.shape), _const_spec(wout.shape)],
        out_specs=pl.BlockSpec((tm, D_MODEL), row),
        out_shape=jax.ShapeDtypeStruct((n, D_MODEL), F32),
        compiler_params=_params(("parallel",)),
        name="merge",
    )(h, g, *branches, wgate, wbr, wout)


def _pack_w_in(w):
    offs = np.cumsum((256, 256, 128, 128, 256, 256, 256, MLA_Q_RANK, MLA_KV_RANK, MLA_ROPE, 256))
    a_in, swa_q, swa_k, swa_v, na_q, na_k, na_v, c_q, c_kv, k_rope, mem_q, gates = jnp.split(w, offs, axis=1)
    sc = HEAD_DIM ** -0.5
    dup = lambda x: jnp.concatenate([x[:, :HEAD_DIM]] * 2 + [x[:, HEAD_DIM:]] * 2, axis=1)
    z = lambda c: jnp.zeros((w.shape[0], c), w.dtype)
    hr = MLA_ROPE // 2
    x1, x2 = k_rope[:, :hr], k_rope[:, hr:]
    kr_a = jnp.concatenate([z(MLA_NOPE), x1, x2, z(LANES - MLA_NOPE - MLA_ROPE)], axis=1)
    kr_b = jnp.concatenate([z(MLA_NOPE), -x2, x1, z(LANES - MLA_NOPE - MLA_ROPE)], axis=1)
    packed = jnp.concatenate([
        dup(swa_k), dup(swa_v), swa_q * sc,
        na_q * sc, na_k, na_v,
        mem_q * sc,
        c_q, z(256 - MLA_Q_RANK), c_kv, kr_a, kr_b], axis=1)
    assert packed.shape[1] == IN_COLS
    return a_in.T.astype(BF16), packed.astype(BF16), gates.astype(BF16)


def _pack_mla(w_q_up, w_kv_up):
    dq = MLA_NOPE + MLA_ROPE
    hr = MLA_ROPE // 2
    zq = lambda c: jnp.zeros((MLA_Q_RANK, c), w_q_up.dtype)
    qa, qb = [], []
    for h in range(MLA_H):
        blk = w_q_up[:, h * dq:(h + 1) * dq]
        nope, x1, x2 = blk[:, :MLA_NOPE], blk[:, MLA_NOPE:MLA_NOPE + hr], blk[:, MLA_NOPE + hr:]
        qa.append(jnp.concatenate([nope, x1, x2, zq(LANES - dq)], axis=1))
        qb.append(jnp.concatenate([zq(MLA_NOPE), -x2, x1, zq(LANES - dq)], axis=1))
    wq = jnp.concatenate(qa + qb, axis=1)
    wq = jnp.pad(wq, ((0, 256 - MLA_Q_RANK), (0, 0)))
    zk = jnp.zeros((MLA_KV_RANK, LANES - MLA_NOPE), w_kv_up.dtype)
    ks, vs = [], []
    for h in range(MLA_H):
        blk = w_kv_up[:, h * (MLA_NOPE + MLA_V):(h + 1) * (MLA_NOPE + MLA_V)]
        ks.append(jnp.concatenate([blk[:, :MLA_NOPE], zk], axis=1))
        vs.append(blk[:, MLA_NOPE:])
    wkv = jnp.concatenate(ks + vs, axis=1)
    return wq.astype(BF16), wkv.astype(BF16)


def _rope_tables(seq):
    half = MLA_ROPE // 2
    inv = ROPE_THETA ** (-jnp.arange(half, dtype=F32) / half)
    ang = jnp.arange(seq, dtype=F32)[:, None] * inv[None, :]
    cos, sin = jnp.cos(ang), jnp.sin(ang)
    one = jnp.ones((seq, MLA_NOPE), F32)
    z0 = jnp.zeros((seq, MLA_NOPE), F32)
    zt = jnp.zeros((seq, LANES - MLA_NOPE - MLA_ROPE), F32)
    return (jnp.concatenate([one, cos, cos, zt], axis=1),
            jnp.concatenate([z0, sin, sin, zt], axis=1))


def kernel(x_prompt, x_sample, mem_prompt, mem_sample,
           ffn1_norm, ffn1_w_gate, ffn1_w_up, ffn1_w_down,
           mix_norm, w_in,
           ssm_lam_re, ssm_lam_im, ssm_log_step, ssm_b_re, ssm_b_im, ssm_c_re, ssm_c_im,
           ssm_d, ssm_w_glu,
           swa_sink, t5_bias, na_rpb,
           mla_q_norm, mla_w_q_up, mla_kv_norm, mla_w_kv_up,
           mem_norm, mem_w_kv, w_branch, w_out,
           ffn2_norm, ffn2_w_gate, ffn2_w_up, ffn2_w_down,
           final_norm):
    groups = []
    for x, mem in ((x_prompt, mem_prompt), (x_sample, mem_sample)):
        bsz, seq, _ = x.shape
        assert seq % TOKEN_TILE == 0 and seq % (NA_ROWS_PER_STEP * GRID_W) == 0
        assert seq % (SSM_T * SUBLANES) == 0
        groups.append(dict(b=bsz, s=seq, h=x.reshape(bsz * seq, D_MODEL), mem=mem))
    max_seq = max(g['s'] for g in groups)
    cos_t, sin_t = _rope_tables(max_seq)
    swa_bias = _swa_bias(t5_bias.astype(F32))
    row = lambda v: v.reshape(1, -1)

    segments, base = [], 0
    for g in groups:
        for _ in range(g['b']):
            segments.append((base, g['s'] // SSM_T))
            base += g['s'] // SSM_T
    segments = tuple(segments)

    for l in range(DEPTH):
        wa_t, w_pack, w_gate = _pack_w_in(w_in[l])
        wq, wkv = _pack_mla(mla_w_q_up[l], mla_w_kv_up[l])
        qn = row(jnp.pad(mla_q_norm[l], (0, 256 - MLA_Q_RANK)))
        na_bias = _na_bias(na_rpb[l].astype(F32))
        vtab, win, wout, apow = _ssm_tables(ssm_lam_re[l], ssm_lam_im[l], ssm_log_step[l],
                                            ssm_b_re[l], ssm_b_im[l], ssm_c_re[l], ssm_c_im[l])
        mt = _ssm_toeplitz(vtab)
        f1 = (row(ffn1_norm[l]), ffn1_w_gate[l].astype(BF16), ffn1_w_up[l].astype(BF16),
              ffn1_w_down[l].astype(BF16))
        f2 = (row(ffn2_norm[l]), ffn2_w_gate[l].astype(BF16), ffn2_w_up[l].astype(BF16),
              ffn2_w_down[l].astype(BF16))
        wbr = w_branch[l].astype(BF16)
        wo = w_out[l].astype(BF16)
        wmem = mem_w_kv[l].astype(BF16)
        wglu_t = ssm_w_glu[l].T.astype(BF16)

        for g in groups:
            g['h'] = _ffn(g['h'], *f1)
            (g['a'], g['swa'], g['na'], g['mq'], g['q'], g['k'], g['v']) = _inproj(
                g['h'], g['s'], row(mix_norm[l]), wa_t, w_pack, wq, wkv, qn, row(mla_kv_norm[l]),
                cos_t, sin_t)

        u4 = [g['a'].reshape(SSM_G, SSM_P, -1, SSM_T) for g in groups]
        y4 = _ssm_conv(u4, mt, win, wout, apow, segments)

        for g, y in zip(groups, y4):
            bsz, seq = g['b'], g['s']
            n = bsz * seq
            r3 = lambda v: v.reshape(bsz, seq, v.shape[-1])
            br0 = _ssm_post(g['a'], y.reshape(BRANCH_W, n), ssm_d[l].reshape(BRANCH_W, 1), wglu_t)
            br1 = _swa(r3(g['swa']), swa_sink[l].astype(F32), swa_bias, bsz, seq)
            br2 = _na(r3(g['na']), na_bias, bsz, seq)
            br3 = _mla(r3(g['q']), r3(g['k']), r3(g['v']), bsz, seq)
            kvm = _memkv(g['mem'], row(mem_norm[l]), wmem)
            br4 = _memattn(r3(g['mq']), kvm, bsz, seq)
            flat = lambda v: v.reshape(n, BRANCH_W)
            g['h'] = _merge(g['h'], row(mix_norm[l]),
                            (br0, flat(br1), flat(br2), flat(br3), flat(br4)),
                            w_gate, wbr, wo)
            g['h'] = _ffn(g['h'], *f2, final_g=row(final_norm) if l == DEPTH - 1 else None)

    return tuple(g['h'].reshape(g['b'], g['s'], D_MODEL) for g in groups)
```

```python
import functools
import math

import numpy as np
import jax
import jax.numpy as jnp
from jax import lax
from jax.experimental import pallas as pl
from jax.experimental.pallas import tpu as pltpu

F32 = jnp.float32
BF16 = jnp.bfloat16

D_MODEL = 1024
DEPTH = 2
GRID_W = 64
D_FF = 2816
HEAD_DIM = 64
BRANCH_W = 256
N_BRANCH = 5
SSM_P = 16
SSM_G = BRANCH_W // SSM_P
SSM_N = 64
SWA_HQ = 4
SWA_HKV = 2
SWA_WIN = 128
SWA_BLK = 128
T5_BUCKETS = 32
T5_MAX_DIST = 128
NA_H = 4
NA_KH = 8
NA_KW = 16
MLA_H = 4
MLA_Q_RANK = 192
MLA_KV_RANK = 128
MLA_NOPE = 64
MLA_ROPE = 32
MLA_V = 64
ROPE_THETA = 10000.0
MEM_H = 4
EPS = 1e-6
NEG = -1e30

LANES = 128
SUBLANES = 8
HALF = LANES // 2
SSM_T = LANES
SSM_TP = SSM_T * SSM_P
SSM_SW = 4 * LANES
FFN_CHUNK = 256
TOKEN_TILE = 512
MLA_TQ = 512
MLA_TK = 2048
NA_GROUP = 4
NA_WIN = NA_KH + NA_GROUP
NA_GROUPS_PER_STEP = 2
VMEM_LIMIT = 56 * 1024 * 1024
LOG2E = math.log2(math.e)
MLA_QSCALE = (MLA_NOPE + MLA_ROPE) ** -0.5 * LOG2E

IN_SWA = (0, 768)
IN_NA = (768, 1536)
IN_MEMQ = (1536, 1792)
IN_LAT = (1792, 2432)
IN_COLS = 2432


def _dot(a, b):
    return jnp.dot(a, b, preferred_element_type=F32)


def _dot_nt(a, b):
    return lax.dot_general(a, b, (((1,), (1,)), ((), ())), preferred_element_type=F32)


def _rms(x, g):
    return x * lax.rsqrt(jnp.mean(x * x, axis=-1, keepdims=True) + EPS) * g


def _sigmoid(x):
    return 1.0 / (1.0 + jnp.exp(-x))


def _params(sem):
    return pltpu.CompilerParams(dimension_semantics=sem, vmem_limit_bytes=VMEM_LIMIT)


def _const_spec(shape):
    nd = len(shape)
    return pl.BlockSpec(shape, lambda *_: (0,) * nd, pipeline_mode=pl.Buffered(1))


def _layer_spec(arr, l):
    nd = arr.ndim - 1
    return pl.BlockSpec((None,) + arr.shape[1:], lambda *_: (l,) + (0,) * nd, pipeline_mode=pl.Buffered(1))


def _lo_lanes():
    return lax.broadcasted_iota(jnp.int32, (1, LANES), 1) < HALF


def _stack_heads(qt):
    lo = _lo_lanes()
    z = jnp.zeros_like(qt)
    return jnp.concatenate([jnp.where(lo, qt, z), jnp.where(lo, z, qt)], axis=0)


def _unstack_heads(o):
    n = o.shape[0] // 2
    return jnp.where(_lo_lanes(), o[:n], o[n:])


def _ffn_kernel(*refs, final):
    if final:
        x_ref, g_ref, wg_ref, wu_ref, wd_ref, fn_ref, o_ref = refs
    else:
        x_ref, g_ref, wg_ref, wu_ref, wd_ref, o_ref = refs
    x = x_ref[...]
    xn = _rms(x, g_ref[...]).astype(BF16)
    acc = jnp.zeros(x.shape, F32)
    for c in range(D_FF // FFN_CHUNK):
        sl = slice(c * FFN_CHUNK, (c + 1) * FFN_CHUNK)
        g = _dot(xn, wg_ref[:, sl])
        u = _dot(xn, wu_ref[:, sl])
        a = (g * _sigmoid(g) * u).astype(BF16)
        acc = acc + _dot(a, wd_ref[sl, :])
    y = x + 0.5 * acc
    if final:
        y = _rms(y, fn_ref[...])
    o_ref[...] = y


def _ffn(x, l, g, wg, wu, wd, final_g=None):
    n = x.shape[0]
    tm = TOKEN_TILE
    final = final_g is not None
    ins = [x, g, wg, wu, wd]
    specs = [pl.BlockSpec((tm, D_MODEL), lambda i: (i, 0))] + [_layer_spec(a, l) for a in ins[1:]]
    if final:
        ins.append(final_g)
        specs.append(_const_spec((1, D_MODEL)))
    return pl.pallas_call(
        functools.partial(_ffn_kernel, final=final),
        grid=(n // tm,),
        in_specs=specs,
        out_specs=pl.BlockSpec((tm, D_MODEL), lambda i: (i, 0)),
        out_shape=jax.ShapeDtypeStruct((n, D_MODEL), F32),
        compiler_params=_params(("parallel",)),
        name="ffn_final" if final else "ffn",
    )(*ins)


def _inproj_kernel(x_ref, g_ref, wa_ref, w_ref, wq_ref, wkv_ref, qn_ref, kvn_ref, cos_ref, sin_ref,
                   a_ref, swa_ref, na_ref, mq_ref, q_ref, k_ref, v_ref):
    u = _rms(x_ref[...], g_ref[...]).astype(BF16)
    a_ref[...] = _dot_nt(wa_ref[...], u)
    swa_ref[...] = _dot(u, w_ref[:, IN_SWA[0]:IN_SWA[1]]).astype(BF16)
    na_ref[...] = _dot(u, w_ref[:, IN_NA[0]:IN_NA[1]]).astype(BF16)
    mq_ref[...] = _dot(u, w_ref[:, IN_MEMQ[0]:IN_MEMQ[1]]).astype(BF16)
    lat = _dot(u, w_ref[:, IN_LAT[0]:IN_LAT[1]])
    cq = lat[:, 0:256]
    ckv = lat[:, 256:384]
    kra = lat[:, 384:512]
    krb = lat[:, 512:640]
    cos = cos_ref[...]
    sin = sin_ref[...]
    ms = jnp.sum(cq * cq, axis=-1, keepdims=True) * (1.0 / MLA_Q_RANK)
    cqn = (cq * lax.rsqrt(ms + EPS) * qn_ref[...]).astype(BF16)
    qab = _dot(cqn, wq_ref[...])
    cos4 = jnp.concatenate([cos] * MLA_H, axis=1)
    sin4 = jnp.concatenate([sin] * MLA_H, axis=1)
    nq = MLA_H * LANES
    q = (qab[:, :nq] * cos4 + qab[:, nq:] * sin4) * MLA_QSCALE
    q_ref[...] = q.astype(BF16)
    ckvn = _rms(ckv, kvn_ref[...]).astype(BF16)
    kv = _dot(ckvn, wkv_ref[...])
    kr = kra * cos + krb * sin
    k_ref[...] = (kv[:, :nq] + jnp.concatenate([kr] * MLA_H, axis=1)).astype(BF16)
    v_ref[...] = kv[:, nq:].astype(BF16)


def _inproj(x, seq_len, l, g, wa_t, w, wq, wkv, qn, kvn, cos_t, sin_t):
    n = x.shape[0]
    tm = TOKEN_TILE
    ns = seq_len // tm
    widths = (768, 768, 256, 512, 512, 256)
    row = lambda i: (i, 0)
    return pl.pallas_call(
        _inproj_kernel,
        grid=(n // tm,),
        in_specs=[pl.BlockSpec((tm, D_MODEL), row)]
                 + [_layer_spec(a, l) for a in (g, wa_t, w, wq, wkv, qn, kvn)]
                 + [pl.BlockSpec((tm, LANES), lambda i: (i % ns, 0)),
                  pl.BlockSpec((tm, LANES), lambda i: (i % ns, 0))],
        out_specs=[pl.BlockSpec((BRANCH_W, tm), lambda i: (0, i))]
                  + [pl.BlockSpec((tm, wd), row) for wd in widths],
        out_shape=[jax.ShapeDtypeStruct((BRANCH_W, n), F32)]
                  + [jax.ShapeDtypeStruct((n, wd), BF16) for wd in widths],
        compiler_params=_params(("parallel",)),
        name="inproj",
    )(x, g, wa_t, w, wq, wkv, qn, kvn, cos_t, sin_t)


def _toeplitz_kernel(v_ref, o_ref):
    rows = o_ref.shape[0]
    n = v_ref.shape[0]
    w = v_ref.shape[1]
    for p in range(n):
        x = jnp.broadcast_to(v_ref[p:p + 1, :], (rows, w))
        r = pltpu.roll(x, 1, 1, stride=1, stride_axis=0)
        o_ref[:, p * (w - rows):(p + 1) * (w - rows)] = r[:, rows:].astype(o_ref.dtype)


def _ssm_toeplitz(vtab):
    g, q, p, w = vtab.shape
    t = w // 2
    return pl.pallas_call(
        _toeplitz_kernel,
        grid=(g, q),
        in_specs=[pl.BlockSpec((None, None, p, w), lambda i, j: (i, j, 0, 0))],
        out_specs=pl.BlockSpec((None, t, p * t), lambda i, j: (i, j, 0)),
        out_shape=jax.ShapeDtypeStruct((g, q * t, p * t), BF16),
        compiler_params=_params(("parallel", "parallel")),
        name="ssm_toeplitz",
    )(vtab)


def _ssm_conv_kernel(*refs, segments, n_in):
    u_refs = refs[:n_in]
    mt_ref, win_ref, wout_ref, ap_ref = refs[n_in:n_in + 4]
    y_refs = refs[n_in + 4:2 * n_in + 4]
    s_scr, x_scr = refs[2 * n_in + 4:]
    u = jnp.concatenate(
        [jnp.concatenate([r[q] for q in range(SSM_P)], axis=1) for r in u_refs], axis=0).astype(BF16)
    s_scr[...] = _dot(u, win_ref[...])
    afr = ap_ref[0:1, :]
    afi = ap_ref[1:2, :]
    abr = ap_ref[2:3, :]
    abi = ap_ref[3:4, :]
    z = jnp.zeros((1, LANES), F32)
    sub = SUBLANES
    for base, nch in segments:
        def body(c, carry, base=base, nch=nch):
            xfr, xfi, xbr, xbi = carry
            rf = pl.multiple_of(base + c * sub, sub)
            rb = pl.multiple_of(base + nch - (c + 1) * sub, sub)
            sf = s_scr[pl.ds(rf, sub), 0:2 * LANES]
            sb = s_scr[pl.ds(rb, sub), 2 * LANES:4 * LANES]
            ofr, ofi, obr, obi = [], [], [None] * sub, [None] * sub
            for k in range(sub):
                ofr.append(xfr)
                ofi.append(xfi)
                xfr, xfi = (afr * xfr - afi * xfi + sf[k:k + 1, 0:LANES],
                            afr * xfi + afi * xfr + sf[k:k + 1, LANES:2 * LANES])
                kb = sub - 1 - k
                obr[kb] = xbr
                obi[kb] = xbi
                xbr, xbi = (abr * xbr - abi * xbi + sb[kb:kb + 1, 0:LANES],
                            abr * xbi + abi * xbr + sb[kb:kb + 1, LANES:2 * LANES])
            x_scr[pl.ds(rf, sub), 0:LANES] = jnp.concatenate(ofr, axis=0)
            x_scr[pl.ds(rf, sub), LANES:2 * LANES] = jnp.concatenate(ofi, axis=0)
            x_scr[pl.ds(rb, sub), 2 * LANES:3 * LANES] = jnp.concatenate(obr, axis=0)
            x_scr[pl.ds(rb, sub), 3 * LANES:4 * LANES] = jnp.concatenate(obi, axis=0)
            return xfr, xfi, xbr, xbi
        assert base % sub == 0 and nch % sub == 0
        lax.fori_loop(0, nch // sub, body, (z, z, z, z))
    y = _dot(u, mt_ref[...]) + _dot(x_scr[...].astype(BF16), wout_ref[...])
    row = 0
    for r in y_refs:
        nr = r.shape[1]
        for p in range(SSM_P):
            r[p] = y[row:row + nr, p * SSM_T:(p + 1) * SSM_T]
        row += nr


def _ssm_conv(u_list, l, mt, win, wout, apow, segments):
    nc = sum(u.shape[2] for u in u_list)
    blk = lambda u: pl.BlockSpec((None, SSM_P, u.shape[2], SSM_T), lambda i: (i, 0, 0, 0))
    w3 = lambda a: pl.BlockSpec((None,) + a.shape[1:], lambda i: (l * SSM_G + i, 0, 0))
    return pl.pallas_call(
        functools.partial(_ssm_conv_kernel, segments=segments, n_in=len(u_list)),
        grid=(SSM_G,),
        in_specs=[blk(u) for u in u_list] + [w3(mt), w3(win), w3(wout), w3(apow)],
        out_specs=[blk(u) for u in u_list],
        out_shape=[jax.ShapeDtypeStruct(u.shape, F32) for u in u_list],
        scratch_shapes=[pltpu.VMEM((nc, SSM_SW), F32), pltpu.VMEM((nc, SSM_SW), F32)],
        compiler_params=_params(("parallel",)),
        name="ssm_conv",
    )(*u_list, mt, win, wout, apow)


def _ssm_post_kernel(a_ref, y_ref, d_ref, w_ref, o_ref):
    y = d_ref[...] * a_ref[...] + y_ref[...]
    c = math.sqrt(2.0 / math.pi)
    gl = y * (0.5 * (1.0 + jnp.tanh(c * (y + 0.044715 * (y * y * y)))))
    o = gl * _sigmoid(_dot(w_ref[...], gl.astype(BF16)))
    o_ref[...] = o.T.astype(BF16)


def _ssm_post(a_t, y_t, l, d, w_t):
    n = a_t.shape[1]
    tm = TOKEN_TILE
    col = lambda i: (0, i)
    return pl.pallas_call(
        _ssm_post_kernel,
        grid=(n // tm,),
        in_specs=[pl.BlockSpec((BRANCH_W, tm), col), pl.BlockSpec((BRANCH_W, tm), col),
                  _layer_spec(d, l), _layer_spec(w_t, l)],
        out_specs=pl.BlockSpec((tm, BRANCH_W), lambda i: (i, 0)),
        out_shape=jax.ShapeDtypeStruct((n, BRANCH_W), BF16),
        compiler_params=_params(("parallel",)),
        name="ssm_post",
    )(a_t, y_t, d, w_t)


def _ssm_tables(lam_re, lam_im, log_step, b_re, b_im, c_re, c_im):
    hp = lax.Precision.HIGHEST
    t, g, n, p = SSM_T, SSM_G, SSM_N, SSM_P
    dt = jnp.exp(log_step)[..., None]
    lr, li = lam_re, lam_im
    mag = jnp.exp(lr * dt)
    a_re = mag * jnp.cos(li * dt)
    a_im = mag * jnp.sin(li * dt)
    den = lr * lr + li * li
    xr = a_re - 1.0
    k_re = (xr * lr + a_im * li) / den
    k_im = (a_im * lr - xr * li) / den
    bb_re = k_re[..., None] * b_re - k_im[..., None] * b_im
    bb_im = k_re[..., None] * b_im + k_im[..., None] * b_re
    kk = jnp.arange(t + 1, dtype=F32)[:, None, None, None]
    pmag = jnp.exp(kk * (lr * dt)[None])
    ang = kk * (li * dt)[None]
    pw_re = pmag * jnp.cos(ang)
    pw_im = pmag * jnp.sin(ang)
    e_re = pw_re[..., None] * bb_re[None] - pw_im[..., None] * bb_im[None]
    e_im = pw_re[..., None] * bb_im[None] + pw_im[..., None] * bb_re[None]
    kd = (jnp.einsum('dgpn,tdgnq->dtgpq', c_re, e_re[:t], precision=hp)
          - jnp.einsum('dgpn,tdgnq->dtgpq', c_im, e_im[:t], precision=hp))
    kf, kb = kd[0], kd[1]
    kfull = jnp.concatenate([kb[:0:-1], (kf[0] + kb[0])[None], kf[1:]], axis=0)
    vtab = jnp.pad(kfull.transpose(1, 3, 2, 0), ((0, 0), (0, 0), (0, 0), (0, 1)))

    def pad_n(x, axis):
        pads = [(0, 0)] * x.ndim
        pads[axis] = (0, LANES - n)
        return jnp.pad(x, pads)

    def in_blk(e):
        return pad_n(e.transpose(1, 3, 0, 2).reshape(g, p * t, n), 2)
    win = jnp.concatenate([in_blk(e_re[t - 1::-1, 0]), in_blk(e_im[t - 1::-1, 0]),
                           in_blk(e_re[:t, 1]), in_blk(e_im[:t, 1])], axis=2)
    pf_re, pf_im = pw_re[1:, 0], pw_im[1:, 0]
    pb_re, pb_im = pw_re[t:0:-1, 1], pw_im[t:0:-1, 1]

    def out_blk(c_r, c_i, p_r, p_i):
        ck_re = c_r[None] * p_r[:, :, None, :] - c_i[None] * p_i[:, :, None, :]
        ck_im = c_r[None] * p_i[:, :, None, :] + c_i[None] * p_r[:, :, None, :]
        f = lambda x: pad_n(x.transpose(1, 3, 2, 0).reshape(g, n, p * t), 1)
        return f(ck_re), f(-ck_im)
    wf_re, wf_im = out_blk(c_re[0], c_im[0], pf_re, pf_im)
    wb_re, wb_im = out_blk(c_re[1], c_im[1], pb_re, pb_im)
    wout = jnp.concatenate([wf_re, wf_im, wb_re, wb_im], axis=1)
    apow = jnp.stack([pw_re[t, 0], pw_im[t, 0], pw_re[t, 1], pw_im[t, 1]], axis=1)
    apow = pad_n(apow, 2)
    return vtab, win.astype(BF16), wout.astype(BF16), apow


def _swa_kernel(sink_ref, q_ref, kv_ref, bias_ref, o_ref, *, l, nb_total, nb_step):
    i = pl.program_id(1)
    top = lax.broadcasted_iota(jnp.int32, (2 * SWA_BLK, 1), 0) < SWA_BLK
    for j in range(nb_step):
        gi = i * nb_step + j
        ps = pl.multiple_of(jnp.maximum(gi - 1, 0) * SWA_BLK, SWA_BLK)
        cs = pl.multiple_of(gi * SWA_BLK, SWA_BLK)
        ns = pl.multiple_of(jnp.minimum(gi + 1, nb_total - 1) * SWA_BLK, SWA_BLK)
        kv = jnp.concatenate([kv_ref[pl.ds(ps, SWA_BLK), :], kv_ref[pl.ds(cs, SWA_BLK), :],
                              kv_ref[pl.ds(ns, SWA_BLK), :]], axis=0)
        variant = jnp.where(gi == 0, 1, jnp.where(gi == nb_total - 1, 2, 0))
        q2 = q_ref[j * SWA_BLK:(j + 1) * SWA_BLK, :]
        for t in range(SWA_HKV):
            kk = kv[:, t * LANES:(t + 1) * LANES]
            vv = kv[:, (SWA_HKV + t) * LANES:(SWA_HKV + t + 1) * LANES]
            sink = jnp.where(top, sink_ref[l, 2 * t], sink_ref[l, 2 * t + 1])
            s = _dot_nt(_stack_heads(q2[:, t * LANES:(t + 1) * LANES]), kk) + bias_ref[variant, t]
            m = jnp.maximum(jnp.max(s, axis=-1, keepdims=True), sink)
            p = jnp.exp2(s - m)
            den = jnp.sum(p, axis=-1, keepdims=True) + jnp.exp2(sink - m)
            o = _dot(p.astype(BF16), vv) * (1.0 / den)
            o_ref[j * SWA_BLK:(j + 1) * SWA_BLK, t * LANES:(t + 1) * LANES] = _unstack_heads(o).astype(BF16)


def _swa(swa, l, sink, bias, bsz, seq):
    tq = TOKEN_TILE
    nb_step = tq // SWA_BLK
    assert seq // SWA_BLK >= 2
    return pl.pallas_call(
        functools.partial(_swa_kernel, l=l, nb_total=seq // SWA_BLK, nb_step=nb_step),
        grid=(bsz, seq // tq),
        in_specs=[pl.BlockSpec(memory_space=pltpu.SMEM),
                  pl.BlockSpec((None, tq, 256), lambda b, i: (b, i, 2)),
                  pl.BlockSpec((None, seq, 512), lambda b, i: (b, 0, 0)),
                  _const_spec(bias.shape)],
        out_specs=pl.BlockSpec((None, tq, BRANCH_W), lambda b, i: (b, i, 0)),
        out_shape=jax.ShapeDtypeStruct((bsz, seq, BRANCH_W), BF16),
        compiler_params=_params(("parallel", "parallel")),
        name="swa",
    )(sink, swa, swa, bias)


def _t5_bucket_np(rel):
    nb = T5_BUCKETS // 2
    max_exact = nb // 2
    ret = (rel > 0).astype(np.int32) * nb
    n = np.abs(rel)
    nf = np.maximum(n, 1).astype(np.float32)
    large = max_exact + (np.log(nf / np.float32(max_exact)) / np.float32(math.log(T5_MAX_DIST / max_exact))
                         * np.float32(nb - max_exact)).astype(np.int32)
    large = np.minimum(large, nb - 1)
    return ret + np.where(n < max_exact, n, large)


def _t5_expand_kernel(v_ref, o_ref):
    rows, w = o_ref.shape[0], v_ref.shape[1]
    x = jnp.broadcast_to(v_ref[...], (rows, w))
    o_ref[...] = pltpu.roll(x, 1, 1, stride=1, stride_axis=0)[:, rows:]


def _swa_bias(t5_bias):
    rel = np.arange(-(2 * SWA_BLK - 1), 2 * SWA_BLK)
    rel = np.concatenate([rel, [2 * SWA_BLK]])
    onehot = jnp.asarray(np.eye(T5_BUCKETS, dtype=np.float32)[_t5_bucket_np(rel)])
    vec = jnp.dot(onehot, t5_bias, precision=lax.Precision.HIGHEST).T
    vec = jnp.where(jnp.asarray(np.abs(rel) <= SWA_WIN)[None], vec, NEG)
    w = vec.shape[1]
    tab = pl.pallas_call(
        _t5_expand_kernel,
        grid=(SWA_HQ,),
        in_specs=[pl.BlockSpec((None, 1, w), lambda h: (h, 0, 0))],
        out_specs=pl.BlockSpec((None, SWA_BLK, 3 * SWA_BLK), lambda h: (h, 0, 0)),
        out_shape=jax.ShapeDtypeStruct((SWA_HQ, SWA_BLK, 3 * SWA_BLK), F32),
        name="t5_expand",
    )(vec.reshape(SWA_HQ, 1, w))
    tab = tab.reshape(SWA_HKV, 2 * SWA_BLK, 3 * SWA_BLK) * LOG2E
    col = np.arange(3 * SWA_BLK)
    first = jnp.where(jnp.asarray(col >= SWA_BLK), tab, NEG)
    last = jnp.where(jnp.asarray(col < 2 * SWA_BLK), tab, NEG)
    return jnp.stack([tab, first, last])


def _na_kernel(q_ref, k_ref, v_ref, bias_ref, o_ref, *, rows, groups_step):
    i = pl.program_id(1)
    nq = NA_GROUP * GRID_W
    nkey = NA_WIN * GRID_W
    for gg in range(groups_step):
        r0 = (i * groups_step + gg) * NA_GROUP
        ws = jnp.clip(r0 - NA_KH // 2, 0, rows - NA_WIN)
        variant = (r0 - ws) // NA_GROUP
        ks = pl.multiple_of(ws * GRID_W, GRID_W)
        kt = k_ref[pl.ds(ks, nkey), :]
        vt = v_ref[pl.ds(ks, nkey), :]
        q2 = q_ref[gg * nq:(gg + 1) * nq, :]
        for t in range(NA_H // 2):
            sl = slice(t * LANES, (t + 1) * LANES)
            s = _dot_nt(_stack_heads(q2[:, sl]), kt[:, sl]) + bias_ref[variant, t]
            m = jnp.max(s, axis=-1, keepdims=True)
            p = jnp.exp2(s - m)
            den = jnp.sum(p, axis=-1, keepdims=True)
            o = _dot(p.astype(BF16), vt[:, sl]) * (1.0 / den)
            o_ref[gg * nq:(gg + 1) * nq, sl] = _unstack_heads(o).astype(BF16)


def _na(na, l, bias, bsz, seq):
    rows = seq // GRID_W
    assert rows >= NA_WIN and rows % NA_GROUP == 0
    gstep = NA_GROUPS_PER_STEP
    tq = gstep * NA_GROUP * GRID_W
    return pl.pallas_call(
        functools.partial(_na_kernel, rows=rows, groups_step=gstep),
        grid=(bsz, seq // tq),
        in_specs=[pl.BlockSpec((None, tq, 256), lambda b, i: (b, i, 0)),
                  pl.BlockSpec((None, seq, 256), lambda b, i: (b, 0, 1)),
                  pl.BlockSpec((None, seq, 256), lambda b, i: (b, 0, 2)),
                  _layer_spec(bias, l)],
        out_specs=pl.BlockSpec((None, tq, BRANCH_W), lambda b, i: (b, i, 0)),
        out_shape=jax.ShapeDtypeStruct((bsz, seq, BRANCH_W), BF16),
        compiler_params=_params(("parallel", "parallel")),
        name="na",
    )(na, na, na, bias)


def _na_bias(rpb):
    hp = lax.Precision.HIGHEST
    cols = np.arange(GRID_W)
    cs = np.clip(cols - NA_KW // 2, 0, GRID_W - NA_KW)
    kc = np.arange(GRID_W)
    valid_c = (kc[None, :] >= cs[:, None]) & (kc[None, :] < cs[:, None] + NA_KW)
    dc = np.clip(kc[None, :] - cols[:, None] + (NA_KW - 1), 0, 2 * NA_KW - 2)
    off = np.arange(3)[:, None, None] * NA_GROUP
    rl = np.arange(NA_GROUP)[None, :, None]
    kl = np.arange(NA_WIN)[None, None, :]
    rs = np.stack([np.zeros(NA_GROUP, np.int64), np.arange(NA_GROUP),
                   np.full(NA_GROUP, NA_WIN - NA_KH)])[:, :, None]
    valid_r = (kl >= rs) & (kl < rs + NA_KH)
    dr = np.clip(kl - off - rl + (NA_KH - 1), 0, 2 * NA_KH - 2)
    oh_c = np.eye(2 * NA_KW - 1, dtype=np.float32)[dc] * valid_c[..., None]
    oh_r = np.eye(2 * NA_KH - 1, dtype=np.float32)[dr] * valid_r[..., None]
    tab = jnp.einsum('vrka,hab,cjb->vhrckj', jnp.asarray(oh_r), rpb, jnp.asarray(oh_c), precision=hp)
    valid = valid_r[:, None, :, None, :, None] & valid_c[None, None, None, :, None, :]
    tab = jnp.where(jnp.asarray(valid), tab * LOG2E, NEG)
    return tab.reshape(3, NA_H // 2, 2 * NA_GROUP * GRID_W, NA_WIN * GRID_W)


def _mla_kernel(q_ref, k_ref, v_ref, o_ref, *, seq, tk):
    tq = q_ref.shape[0]

    def body(j, carry):
        ks = pl.multiple_of(j * tk, tk)
        out = []
        for h in range(MLA_H):
            m, l, acc = carry[3 * h:3 * h + 3]
            hs = slice(h * LANES, (h + 1) * LANES)
            ts = slice((h // 2) * LANES, (h // 2 + 1) * LANES)
            s = _dot_nt(q_ref[:, hs], k_ref[pl.ds(ks, tk), hs])
            m_new = jnp.maximum(m, jnp.max(s, axis=-1, keepdims=True))
            alpha = jnp.exp2(m - m_new)
            p = jnp.exp2(s - m_new)
            l = alpha * l + jnp.sum(p, axis=-1, keepdims=True)
            acc = alpha * acc + _dot(p.astype(BF16), v_ref[pl.ds(ks, tk), ts])
            out += [m_new, l, acc]
        return tuple(out)

    init = (jnp.full((tq, 1), -jnp.inf, F32), jnp.zeros((tq, 1), F32), jnp.zeros((tq, LANES), F32)) * MLA_H
    res = lax.fori_loop(0, seq // tk, body, init)
    lo = _lo_lanes()
    for t in range(MLA_H // 2):
        oa = res[6 * t + 2] * (1.0 / res[6 * t + 1])
        ob = res[6 * t + 5] * (1.0 / res[6 * t + 4])
        o_ref[:, t * LANES:(t + 1) * LANES] = jnp.where(lo, oa, ob).astype(BF16)


def _mla(q, k, v, bsz, seq):
    tq = MLA_TQ
    tk = min(MLA_TK, seq)
    assert seq % tk == 0 and seq % tq == 0
    return pl.pallas_call(
        functools.partial(_mla_kernel, seq=seq, tk=tk),
        grid=(bsz, seq // tq),
        in_specs=[pl.BlockSpec((None, tq, MLA_H * LANES), lambda b, i: (b, i, 0)),
                  pl.BlockSpec((None, seq, MLA_H * LANES), lambda b, i: (b, 0, 0)),
                  pl.BlockSpec((None, seq, BRANCH_W), lambda b, i: (b, 0, 0))],
        out_specs=pl.BlockSpec((None, tq, BRANCH_W), lambda b, i: (b, i, 0)),
        out_shape=jax.ShapeDtypeStruct((bsz, seq, BRANCH_W), BF16),
        compiler_params=_params(("parallel", "parallel")),
        name="mla",
    )(q, k, v)


def _memkv_kernel(m_ref, g_ref, w_ref, o_ref):
    o_ref[...] = _dot(_rms(m_ref[...], g_ref[...]).astype(BF16), w_ref[...]).astype(BF16)


def _memkv(mem, l, g, w):
    bsz, nm, _ = mem.shape
    wd = w.shape[-1]
    return pl.pallas_call(
        _memkv_kernel,
        grid=(bsz,),
        in_specs=[pl.BlockSpec((None, nm, D_MODEL), lambda b: (b, 0, 0)),
                  _layer_spec(g, l), _layer_spec(w, l)],
        out_specs=pl.BlockSpec((None, nm, wd), lambda b: (b, 0, 0)),
        out_shape=jax.ShapeDtypeStruct((bsz, nm, wd), BF16),
        compiler_params=_params(("parallel",)),
        name="memkv",
    )(mem, g, w)


def _memattn_kernel(q_ref, kv_ref, o_ref):
    hd = MEM_H * HEAD_DIM
    for t in range(MEM_H // 2):
        sl = slice(t * LANES, (t + 1) * LANES)
        s = _dot_nt(_stack_heads(q_ref[:, sl]), kv_ref[:, sl])
        m = jnp.max(s, axis=-1, keepdims=True)
        p = jnp.exp2(s - m)
        den = jnp.sum(p, axis=-1, keepdims=True)
        o = _dot(p.astype(BF16), kv_ref[:, hd + t * LANES:hd + (t + 1) * LANES]) * (1.0 / den)
        o_ref[:, sl] = _unstack_heads(o).astype(BF16)


def _memattn(q, kv, bsz, seq):
    tq = TOKEN_TILE
    nm, wd = kv.shape[1], kv.shape[2]
    return pl.pallas_call(
        _memattn_kernel,
        grid=(bsz, seq // tq),
        in_specs=[pl.BlockSpec((None, tq, BRANCH_W), lambda b, i: (b, i, 0)),
                  pl.BlockSpec((None, nm, wd), lambda b, i: (b, 0, 0))],
        out_specs=pl.BlockSpec((None, tq, BRANCH_W), lambda b, i: (b, i, 0)),
        out_shape=jax.ShapeDtypeStruct((bsz, seq, BRANCH_W), BF16),
        compiler_params=_params(("parallel", "parallel")),
        name="memattn",
    )(q, kv)


def _merge_kernel(h_ref, g_ref, b0, b1, b2, b3, b4, wg_ref, wb_ref, wo_ref, o_ref):
    h = h_ref[...]
    u = _rms(h, g_ref[...]).astype(BF16)
    merged = None
    for n, br in enumerate((b0, b1, b2, b3, b4)):
        gate = _sigmoid(_dot(u, wg_ref[:, n * D_MODEL:(n + 1) * D_MODEL]))
        term = gate * _dot(br[...], wb_ref[n])
        merged = term if merged is None else merged + term
    o_ref[...] = h + _dot(merged.astype(BF16), wo_ref[...])


def _merge(h, l, g, branches, wgate, wbr, wout):
    n = h.shape[0]
    tm = TOKEN_TILE
    row = lambda i: (i, 0)
    return pl.pallas_call(
        _merge_kernel,
        grid=(n // tm,),
        in_specs=[pl.BlockSpec((tm, D_MODEL), row), _layer_spec(g, l)]
                 + [pl.BlockSpec((tm, BRANCH_W), row)] * N_BRANCH
                 + [_layer_spec(wgate, l), _layer_spec(wbr, l), _layer_spec(wout, l)],
        out_specs=pl.BlockSpec((tm, D_MODEL), row),
        out_shape=jax.ShapeDtypeStruct((n, D_MODEL), F32),
        compiler_params=_params(("parallel",)),
        name="merge",
    )(h, g, *branches, wgate, wbr, wout)


def _pack_w_in(w):
    offs = np.cumsum((256, 256, 128, 128, 256, 256, 256, MLA_Q_RANK, MLA_KV_RANK, MLA_ROPE, 256))
    a_in, swa_q, swa_k, swa_v, na_q, na_k, na_v, c_q, c_kv, k_rope, mem_q, gates = jnp.split(w, offs, axis=1)
    sc = HEAD_DIM ** -0.5 * LOG2E
    dup = lambda x: jnp.concatenate([x[:, :HEAD_DIM]] * 2 + [x[:, HEAD_DIM:]] * 2, axis=1)
    z = lambda c: jnp.zeros((w.shape[0], c), w.dtype)
    hr = MLA_ROPE // 2
    x1, x2 = k_rope[:, :hr], k_rope[:, hr:]
    kr_a = jnp.concatenate([z(MLA_NOPE), x1, x2, z(LANES - MLA_NOPE - MLA_ROPE)], axis=1)
    kr_b = jnp.concatenate([z(MLA_NOPE), -x2, x1, z(LANES - MLA_NOPE - MLA_ROPE)], axis=1)
    packed = jnp.concatenate([
        dup(swa_k), dup(swa_v), swa_q * sc,
        na_q * sc, na_k, na_v,
        mem_q * sc,
        c_q, z(256 - MLA_Q_RANK), c_kv, kr_a, kr_b], axis=1)
    assert packed.shape[1] == IN_COLS
    return a_in.T.astype(BF16), packed.astype(BF16), gates.astype(BF16)


def _pack_mla(w_q_up, w_kv_up):
    dq = MLA_NOPE + MLA_ROPE
    hr = MLA_ROPE // 2
    zq = lambda c: jnp.zeros((MLA_Q_RANK, c), w_q_up.dtype)
    qa, qb = [], []
    for h in range(MLA_H):
        blk = w_q_up[:, h * dq:(h + 1) * dq]
        nope, x1, x2 = blk[:, :MLA_NOPE], blk[:, MLA_NOPE:MLA_NOPE + hr], blk[:, MLA_NOPE + hr:]
        qa.append(jnp.concatenate([nope, x1, x2, zq(LANES - dq)], axis=1))
        qb.append(jnp.concatenate([zq(MLA_NOPE), -x2, x1, zq(LANES - dq)], axis=1))
    wq = jnp.concatenate(qa + qb, axis=1)
    wq = jnp.pad(wq, ((0, 256 - MLA_Q_RANK), (0, 0)))
    zk = jnp.zeros((MLA_KV_RANK, LANES - MLA_NOPE), w_kv_up.dtype)
    ks, vs = [], []
    for h in range(MLA_H):
        blk = w_kv_up[:, h * (MLA_NOPE + MLA_V):(h + 1) * (MLA_NOPE + MLA_V)]
        ks.append(jnp.concatenate([blk[:, :MLA_NOPE], zk], axis=1))
        vs.append(blk[:, MLA_NOPE:])
    wkv = jnp.concatenate(ks + vs, axis=1)
    return wq.astype(BF16), wkv.astype(BF16)


def _rope_tables(seq):
    half = MLA_ROPE // 2
    inv = ROPE_THETA ** (-jnp.arange(half, dtype=F32) / half)
    ang = jnp.arange(seq, dtype=F32)[:, None] * inv[None, :]
    cos, sin = jnp.cos(ang), jnp.sin(ang)
    one = jnp.ones((seq, MLA_NOPE), F32)
    z0 = jnp.zeros((seq, MLA_NOPE), F32)
    zt = jnp.zeros((seq, LANES - MLA_NOPE - MLA_ROPE), F32)
    return (jnp.concatenate([one, cos, cos, zt], axis=1),
            jnp.concatenate([z0, sin, sin, zt], axis=1))


def kernel(x_prompt, x_sample, mem_prompt, mem_sample,
           ffn1_norm, ffn1_w_gate, ffn1_w_up, ffn1_w_down,
           mix_norm, w_in,
           ssm_lam_re, ssm_lam_im, ssm_log_step, ssm_b_re, ssm_b_im, ssm_c_re, ssm_c_im,
           ssm_d, ssm_w_glu,
           swa_sink, t5_bias, na_rpb,
           mla_q_norm, mla_w_q_up, mla_kv_norm, mla_w_kv_up,
           mem_norm, mem_w_kv, w_branch, w_out,
           ffn2_norm, ffn2_w_gate, ffn2_w_up, ffn2_w_down,
           final_norm):
    groups = []
    for x, mem in ((x_prompt, mem_prompt), (x_sample, mem_sample)):
        bsz, seq, _ = x.shape
        assert seq % TOKEN_TILE == 0 and seq % (NA_GROUPS_PER_STEP * NA_GROUP * GRID_W) == 0
        assert seq % (SSM_T * SUBLANES) == 0
        groups.append(dict(b=bsz, s=seq, h=x.reshape(bsz * seq, D_MODEL), mem=mem))
    max_seq = max(g['s'] for g in groups)
    cos_t, sin_t = _rope_tables(max_seq)
    swa_bias = _swa_bias(t5_bias.astype(F32))

    segments, base = [], 0
    for g in groups:
        for _ in range(g['b']):
            segments.append((base, g['s'] // SSM_T))
            base += g['s'] // SSM_T
    segments = tuple(segments)

    nl = w_in.shape[0]
    rows3 = lambda v: v.reshape(nl, 1, -1)
    bf = lambda v: v.astype(BF16)
    wa_t, w_pack, w_gate = jax.vmap(_pack_w_in)(w_in)
    wq, wkv = jax.vmap(_pack_mla)(mla_w_q_up, mla_w_kv_up)
    qn = rows3(jnp.pad(mla_q_norm, ((0, 0), (0, 256 - MLA_Q_RANK))))
    na_bias = jax.vmap(_na_bias)(na_rpb.astype(F32))
    vtab, win, wout, apow = jax.vmap(_ssm_tables)(ssm_lam_re, ssm_lam_im, ssm_log_step,
                                                  ssm_b_re, ssm_b_im, ssm_c_re, ssm_c_im)
    lg = lambda v: v.reshape((nl * SSM_G,) + v.shape[2:])
    mt = _ssm_toeplitz(lg(vtab))
    win, wout, apow = lg(win), lg(wout), lg(apow)
    f1 = (rows3(ffn1_norm), bf(ffn1_w_gate), bf(ffn1_w_up), bf(ffn1_w_down))
    f2 = (rows3(ffn2_norm), bf(ffn2_w_gate), bf(ffn2_w_up), bf(ffn2_w_down))
    mixn, kvn, memn = rows3(mix_norm), rows3(mla_kv_norm), rows3(mem_norm)
    wbr, wo, wmem = bf(w_branch), bf(w_out), bf(mem_w_kv)
    wglu_t = bf(ssm_w_glu.transpose(0, 2, 1))
    ssm_d3 = ssm_d.reshape(nl, BRANCH_W, 1)
    sink = swa_sink.astype(F32) * LOG2E
    final_g = final_norm.reshape(1, -1)

    for l in range(DEPTH):
        for g in groups:
            g['h'] = _ffn(g['h'], l, *f1)
            (g['a'], g['swa'], g['na'], g['mq'], g['q'], g['k'], g['v']) = _inproj(
                g['h'], g['s'], l, mixn, wa_t, w_pack, wq, wkv, qn, kvn, cos_t, sin_t)

        u4 = [g['a'].reshape(SSM_G, SSM_P, -1, SSM_T) for g in groups]
        y4 = _ssm_conv(u4, l, mt, win, wout, apow, segments)

        for g, y in zip(groups, y4):
            bsz, seq = g['b'], g['s']
            n = bsz * seq
            r3 = lambda v: v.reshape(bsz, seq, v.shape[-1])
            br0 = _ssm_post(g['a'], y.reshape(BRANCH_W, n), l, ssm_d3, wglu_t)
            br1 = _swa(r3(g['swa']), l, sink, swa_bias, bsz, seq)
            br2 = _na(r3(g['na']), l, na_bias, bsz, seq)
            br3 = _mla(r3(g['q']), r3(g['k']), r3(g['v']), bsz, seq)
            kvm = _memkv(g['mem'], l, memn, wmem)
            br4 = _memattn(r3(g['mq']), kvm, bsz, seq)
            flat = lambda v: v.reshape(n, BRANCH_W)
            g['h'] = _merge(g['h'], l, mixn,
                            (br0, flat(br1), flat(br2), flat(br3), flat(br4)),
                            w_gate, wbr, wo)
            g['h'] = _ffn(g['h'], l, *f2, final_g=final_g if l == DEPTH - 1 else None)

    return tuple(g['h'].reshape(g['b'], g['s'], D_MODEL) for g in groups)
```

```python
import functools
import math

import numpy as np
import jax
import jax.numpy as jnp
from jax import lax
from jax.experimental import pallas as pl
from jax.experimental.pallas import tpu as pltpu

F32 = jnp.float32
BF16 = jnp.bfloat16

D_MODEL = 1024
DEPTH = 2
GRID_W = 64
D_FF = 2816
HEAD_DIM = 64
BRANCH_W = 256
N_BRANCH = 5
SSM_P = 16
SSM_G = BRANCH_W // SSM_P
SSM_N = 64
SWA_HQ = 4
SWA_HKV = 2
SWA_WIN = 128
SWA_BLK = 128
T5_BUCKETS = 32
T5_MAX_DIST = 128
NA_H = 4
NA_KH = 8
NA_KW = 16
MLA_H = 4
MLA_Q_RANK = 192
MLA_KV_RANK = 128
MLA_NOPE = 64
MLA_ROPE = 32
MLA_V = 64
ROPE_THETA = 10000.0
MEM_H = 4
EPS = 1e-6
NEG = -1e30

LANES = 128
SUBLANES = 8
HALF = LANES // 2
SSM_T = LANES
SSM_TP = SSM_T * SSM_P
SSM_SW = 4 * LANES
TOEPLITZ_Q_PER_STEP = 4
FFN_CHUNK = 256
TOKEN_TILE = 512
MLA_TQ = 512
MLA_TK = 2048
NA_GROUP = 4
NA_WIN = NA_KH + NA_GROUP
NA_GROUPS_PER_STEP = 2
VMEM_LIMIT = 56 * 1024 * 1024
LOG2E = math.log2(math.e)
MLA_QSCALE = (MLA_NOPE + MLA_ROPE) ** -0.5 * LOG2E

IN_SWA = (0, 768)
IN_NA = (768, 1536)
IN_MEMQ = (1536, 1792)
IN_LAT = (1792, 2432)
IN_COLS = 2432


def _dot(a, b):
    return jnp.dot(a, b, preferred_element_type=F32)


def _dot_nt(a, b):
    return lax.dot_general(a, b, (((1,), (1,)), ((), ())), preferred_element_type=F32)


def _rms(x, g):
    return x * lax.rsqrt(jnp.mean(x * x, axis=-1, keepdims=True) + EPS) * g


def _sigmoid(x):
    return 1.0 / (1.0 + jnp.exp(-x))


def _params(sem):
    return pltpu.CompilerParams(dimension_semantics=sem, vmem_limit_bytes=VMEM_LIMIT)


def _const_spec(shape):
    nd = len(shape)
    return pl.BlockSpec(shape, lambda *_: (0,) * nd, pipeline_mode=pl.Buffered(1))


def _layer_spec(arr, l):
    nd = arr.ndim - 1
    return pl.BlockSpec((None,) + arr.shape[1:], lambda *_: (l,) + (0,) * nd, pipeline_mode=pl.Buffered(1))


def _lo_lanes():
    return lax.broadcasted_iota(jnp.int32, (1, LANES), 1) < HALF


def _stack_heads(qt):
    lo = _lo_lanes()
    z = jnp.zeros_like(qt)
    return jnp.concatenate([jnp.where(lo, qt, z), jnp.where(lo, z, qt)], axis=0)


def _unstack_heads(o):
    n = o.shape[0] // 2
    return jnp.where(_lo_lanes(), o[:n], o[n:])


def _ffn_kernel(*refs, final):
    if final:
        x_ref, g_ref, wg_ref, wu_ref, wd_ref, fn_ref, o_ref = refs
    else:
        x_ref, g_ref, wg_ref, wu_ref, wd_ref, o_ref = refs
    x = x_ref[...]
    xn = _rms(x, g_ref[...]).astype(BF16)
    acc = jnp.zeros(x.shape, F32)
    for c in range(D_FF // FFN_CHUNK):
        sl = slice(c * FFN_CHUNK, (c + 1) * FFN_CHUNK)
        g = _dot(xn, wg_ref[:, sl])
        u = _dot(xn, wu_ref[:, sl])
        a = (g * _sigmoid(g) * u).astype(BF16)
        acc = acc + _dot(a, wd_ref[sl, :])
    y = x + 0.5 * acc
    if final:
        y = _rms(y, fn_ref[...])
    o_ref[...] = y


def _ffn(x, l, g, wg, wu, wd, final_g=None):
    n = x.shape[0]
    tm = TOKEN_TILE
    final = final_g is not None
    ins = [x, g, wg, wu, wd]
    specs = [pl.BlockSpec((tm, D_MODEL), lambda i: (i, 0))] + [_layer_spec(a, l) for a in ins[1:]]
    if final:
        ins.append(final_g)
        specs.append(_const_spec((1, D_MODEL)))
    return pl.pallas_call(
        functools.partial(_ffn_kernel, final=final),
        grid=(n // tm,),
        in_specs=specs,
        out_specs=pl.BlockSpec((tm, D_MODEL), lambda i: (i, 0)),
        out_shape=jax.ShapeDtypeStruct((n, D_MODEL), F32),
        compiler_params=_params(("parallel",)),
        name="ffn_final" if final else "ffn",
    )(*ins)


def _inproj_kernel(x_ref, g_ref, wa_ref, w_ref, wq_ref, wkv_ref, qn_ref, kvn_ref, cos_ref, sin_ref,
                   a_ref, swa_ref, na_ref, mq_ref, q_ref, k_ref, v_ref):
    u = _rms(x_ref[...], g_ref[...]).astype(BF16)
    a_ref[...] = _dot_nt(wa_ref[...], u)
    swa_ref[...] = _dot(u, w_ref[:, IN_SWA[0]:IN_SWA[1]]).astype(BF16)
    na_ref[...] = _dot(u, w_ref[:, IN_NA[0]:IN_NA[1]]).astype(BF16)
    mq_ref[...] = _dot(u, w_ref[:, IN_MEMQ[0]:IN_MEMQ[1]]).astype(BF16)
    lat = _dot(u, w_ref[:, IN_LAT[0]:IN_LAT[1]])
    cq = lat[:, 0:256]
    ckv = lat[:, 256:384]
    kra = lat[:, 384:512]
    krb = lat[:, 512:640]
    cos = cos_ref[...]
    sin = sin_ref[...]
    ms = jnp.sum(cq * cq, axis=-1, keepdims=True) * (1.0 / MLA_Q_RANK)
    cqn = (cq * lax.rsqrt(ms + EPS) * qn_ref[...]).astype(BF16)
    qab = _dot(cqn, wq_ref[...])
    cos4 = jnp.concatenate([cos] * MLA_H, axis=1)
    sin4 = jnp.concatenate([sin] * MLA_H, axis=1)
    nq = MLA_H * LANES
    q = (qab[:, :nq] * cos4 + qab[:, nq:] * sin4) * MLA_QSCALE
    q_ref[...] = q.astype(BF16)
    ckvn = _rms(ckv, kvn_ref[...]).astype(BF16)
    kv = _dot(ckvn, wkv_ref[...])
    kr = kra * cos + krb * sin
    k_ref[...] = (kv[:, :nq] + jnp.concatenate([kr] * MLA_H, axis=1)).astype(BF16)
    v_ref[...] = kv[:, nq:].astype(BF16)


def _inproj(x, seq_len, l, g, wa_t, w, wq, wkv, qn, kvn, cos_t, sin_t):
    n = x.shape[0]
    tm = TOKEN_TILE
    ns = seq_len // tm
    widths = (768, 768, 256, 512, 512, 256)
    row = lambda i: (i, 0)
    return pl.pallas_call(
        _inproj_kernel,
        grid=(n // tm,),
        in_specs=[pl.BlockSpec((tm, D_MODEL), row)]
                 + [_layer_spec(a, l) for a in (g, wa_t, w, wq, wkv, qn, kvn)]
                 + [pl.BlockSpec((tm, LANES), lambda i: (i % ns, 0)),
                  pl.BlockSpec((tm, LANES), lambda i: (i % ns, 0))],
        out_specs=[pl.BlockSpec((BRANCH_W, tm), lambda i: (0, i))]
                  + [pl.BlockSpec((tm, wd), row) for wd in widths],
        out_shape=[jax.ShapeDtypeStruct((BRANCH_W, n), F32)]
                  + [jax.ShapeDtypeStruct((n, wd), BF16) for wd in widths],
        compiler_params=_params(("parallel",)),
        name="inproj",
    )(x, g, wa_t, w, wq, wkv, qn, kvn, cos_t, sin_t)


def _toeplitz_kernel(v_ref, o_ref):
    nq, n, w = v_ref.shape
    rows = o_ref.shape[0] // nq
    for q in range(nq):
        for p in range(n):
            x = jnp.broadcast_to(v_ref[q, p:p + 1, :], (rows, w))
            r = pltpu.roll(x, 1, 1, stride=1, stride_axis=0)
            o_ref[q * rows:(q + 1) * rows, p * (w - rows):(p + 1) * (w - rows)] = r[:, rows:].astype(o_ref.dtype)


def _ssm_toeplitz(vtab):
    g, q, p, w = vtab.shape
    t = w // 2
    qb = TOEPLITZ_Q_PER_STEP
    assert q % qb == 0
    return pl.pallas_call(
        _toeplitz_kernel,
        grid=(g, q // qb),
        in_specs=[pl.BlockSpec((None, qb, p, w), lambda i, j: (i, j, 0, 0))],
        out_specs=pl.BlockSpec((None, qb * t, p * t), lambda i, j: (i, j, 0)),
        out_shape=jax.ShapeDtypeStruct((g, q * t, p * t), BF16),
        compiler_params=_params(("parallel", "parallel")),
        name="ssm_toeplitz",
    )(vtab)


def _ssm_conv_kernel(*refs, segments, n_in):
    u_refs = refs[:n_in]
    mt_ref, win_ref, wout_ref, ap_ref = refs[n_in:n_in + 4]
    y_refs = refs[n_in + 4:2 * n_in + 4]
    s_scr, x_scr = refs[2 * n_in + 4:]
    u = jnp.concatenate(
        [jnp.concatenate([r[q] for q in range(SSM_P)], axis=1) for r in u_refs], axis=0).astype(BF16)
    s_scr[...] = _dot(u, win_ref[...])
    afr = ap_ref[0:1, :]
    afi = ap_ref[1:2, :]
    abr = ap_ref[2:3, :]
    abi = ap_ref[3:4, :]
    z = jnp.zeros((1, LANES), F32)
    sub = SUBLANES
    for base, nch in segments:
        def body(c, carry, base=base, nch=nch):
            xfr, xfi, xbr, xbi = carry
            rf = pl.multiple_of(base + c * sub, sub)
            rb = pl.multiple_of(base + nch - (c + 1) * sub, sub)
            sf = s_scr[pl.ds(rf, sub), 0:2 * LANES]
            sb = s_scr[pl.ds(rb, sub), 2 * LANES:4 * LANES]
            ofr, ofi, obr, obi = [], [], [None] * sub, [None] * sub
            for k in range(sub):
                ofr.append(xfr)
                ofi.append(xfi)
                xfr, xfi = (afr * xfr - afi * xfi + sf[k:k + 1, 0:LANES],
                            afr * xfi + afi * xfr + sf[k:k + 1, LANES:2 * LANES])
                kb = sub - 1 - k
                obr[kb] = xbr
                obi[kb] = xbi
                xbr, xbi = (abr * xbr - abi * xbi + sb[kb:kb + 1, 0:LANES],
                            abr * xbi + abi * xbr + sb[kb:kb + 1, LANES:2 * LANES])
            x_scr[pl.ds(rf, sub), 0:LANES] = jnp.concatenate(ofr, axis=0)
            x_scr[pl.ds(rf, sub), LANES:2 * LANES] = jnp.concatenate(ofi, axis=0)
            x_scr[pl.ds(rb, sub), 2 * LANES:3 * LANES] = jnp.concatenate(obr, axis=0)
            x_scr[pl.ds(rb, sub), 3 * LANES:4 * LANES] = jnp.concatenate(obi, axis=0)
            return xfr, xfi, xbr, xbi
        assert base % sub == 0 and nch % sub == 0
        lax.fori_loop(0, nch // sub, body, (z, z, z, z))
    y = _dot(u, mt_ref[...]) + _dot(x_scr[...].astype(BF16), wout_ref[...])
    row = 0
    for r in y_refs:
        nr = r.shape[1]
        for p in range(SSM_P):
            r[p] = y[row:row + nr, p * SSM_T:(p + 1) * SSM_T]
        row += nr


def _ssm_conv(u_list, l, mt, win, wout, apow, segments):
    nc = sum(u.shape[2] for u in u_list)
    blk = lambda u: pl.BlockSpec((None, SSM_P, u.shape[2], SSM_T), lambda i: (i, 0, 0, 0))
    w3 = lambda a: pl.BlockSpec((None,) + a.shape[1:], lambda i: (l * SSM_G + i, 0, 0))
    return pl.pallas_call(
        functools.partial(_ssm_conv_kernel, segments=segments, n_in=len(u_list)),
        grid=(SSM_G,),
        in_specs=[blk(u) for u in u_list] + [w3(mt), w3(win), w3(wout), w3(apow)],
        out_specs=[blk(u) for u in u_list],
        out_shape=[jax.ShapeDtypeStruct(u.shape, F32) for u in u_list],
        scratch_shapes=[pltpu.VMEM((nc, SSM_SW), F32), pltpu.VMEM((nc, SSM_SW), F32)],
        compiler_params=_params(("parallel",)),
        name="ssm_conv",
    )(*u_list, mt, win, wout, apow)


def _ssm_post_kernel(a_ref, y_ref, d_ref, w_ref, o_ref):
    y = d_ref[...] * a_ref[...] + y_ref[...]
    c = math.sqrt(2.0 / math.pi)
    gl = y * (0.5 * (1.0 + jnp.tanh(c * (y + 0.044715 * (y * y * y)))))
    o = gl * _sigmoid(_dot(w_ref[...], gl.astype(BF16)))
    o_ref[...] = o.T.astype(BF16)


def _ssm_post(a_t, y_t, l, d, w_t):
    n = a_t.shape[1]
    tm = TOKEN_TILE
    col = lambda i: (0, i)
    return pl.pallas_call(
        _ssm_post_kernel,
        grid=(n // tm,),
        in_specs=[pl.BlockSpec((BRANCH_W, tm), col), pl.BlockSpec((BRANCH_W, tm), col),
                  _layer_spec(d, l), _layer_spec(w_t, l)],
        out_specs=pl.BlockSpec((tm, BRANCH_W), lambda i: (i, 0)),
        out_shape=jax.ShapeDtypeStruct((n, BRANCH_W), BF16),
        compiler_params=_params(("parallel",)),
        name="ssm_post",
    )(a_t, y_t, d, w_t)


def _ssm_tables(lam_re, lam_im, log_step, b_re, b_im, c_re, c_im):
    hp = lax.Precision.HIGHEST
    t, g, n, p = SSM_T, SSM_G, SSM_N, SSM_P
    dt = jnp.exp(log_step)[..., None]
    lr, li = lam_re, lam_im
    mag = jnp.exp(lr * dt)
    a_re = mag * jnp.cos(li * dt)
    a_im = mag * jnp.sin(li * dt)
    den = lr * lr + li * li
    xr = a_re - 1.0
    k_re = (xr * lr + a_im * li) / den
    k_im = (a_im * lr - xr * li) / den
    bb_re = k_re[..., None] * b_re - k_im[..., None] * b_im
    bb_im = k_re[..., None] * b_im + k_im[..., None] * b_re
    kk = jnp.arange(t + 1, dtype=F32)[:, None, None, None]
    pmag = jnp.exp(kk * (lr * dt)[None])
    ang = kk * (li * dt)[None]
    pw_re = pmag * jnp.cos(ang)
    pw_im = pmag * jnp.sin(ang)
    e_re = pw_re[..., None] * bb_re[None] - pw_im[..., None] * bb_im[None]
    e_im = pw_re[..., None] * bb_im[None] + pw_im[..., None] * bb_re[None]
    kd = (jnp.einsum('dgpn,tdgnq->dtgpq', c_re, e_re[:t], precision=hp)
          - jnp.einsum('dgpn,tdgnq->dtgpq', c_im, e_im[:t], precision=hp))
    kf, kb = kd[0], kd[1]
    kfull = jnp.concatenate([kb[:0:-1], (kf[0] + kb[0])[None], kf[1:]], axis=0)
    vtab = jnp.pad(kfull.transpose(1, 3, 2, 0), ((0, 0), (0, 0), (0, 0), (0, 1)))

    def pad_n(x, axis):
        pads = [(0, 0)] * x.ndim
        pads[axis] = (0, LANES - n)
        return jnp.pad(x, pads)

    def in_blk(e):
        return pad_n(e.transpose(1, 3, 0, 2).reshape(g, p * t, n), 2)
    win = jnp.concatenate([in_blk(e_re[t - 1::-1, 0]), in_blk(e_im[t - 1::-1, 0]),
                           in_blk(e_re[:t, 1]), in_blk(e_im[:t, 1])], axis=2)
    pf_re, pf_im = pw_re[1:, 0], pw_im[1:, 0]
    pb_re, pb_im = pw_re[t:0:-1, 1], pw_im[t:0:-1, 1]

    def out_blk(c_r, c_i, p_r, p_i):
        ck_re = c_r[None] * p_r[:, :, None, :] - c_i[None] * p_i[:, :, None, :]
        ck_im = c_r[None] * p_i[:, :, None, :] + c_i[None] * p_r[:, :, None, :]
        f = lambda x: pad_n(x.transpose(1, 3, 2, 0).reshape(g, n, p * t), 1)
        return f(ck_re), f(-ck_im)
    wf_re, wf_im = out_blk(c_re[0], c_im[0], pf_re, pf_im)
    wb_re, wb_im = out_blk(c_re[1], c_im[1], pb_re, pb_im)
    wout = jnp.concatenate([wf_re, wf_im, wb_re, wb_im], axis=1)
    apow = jnp.stack([pw_re[t, 0], pw_im[t, 0], pw_re[t, 1], pw_im[t, 1]], axis=1)
    apow = pad_n(apow, 2)
    return vtab, win.astype(BF16), wout.astype(BF16), apow


def _swa_kernel(sink_ref, q_ref, kv_ref, bias_ref, o_ref, *, l, nb_total, nb_step):
    i = pl.program_id(1)
    top = lax.broadcasted_iota(jnp.int32, (2 * SWA_BLK, 1), 0) < SWA_BLK
    for j in range(nb_step):
        gi = i * nb_step + j
        ps = pl.multiple_of(jnp.maximum(gi - 1, 0) * SWA_BLK, SWA_BLK)
        cs = pl.multiple_of(gi * SWA_BLK, SWA_BLK)
        ns = pl.multiple_of(jnp.minimum(gi + 1, nb_total - 1) * SWA_BLK, SWA_BLK)
        kv = jnp.concatenate([kv_ref[pl.ds(ps, SWA_BLK), :], kv_ref[pl.ds(cs, SWA_BLK), :],
                              kv_ref[pl.ds(ns, SWA_BLK), :]], axis=0)
        variant = jnp.where(gi == 0, 1, jnp.where(gi == nb_total - 1, 2, 0))
        q2 = q_ref[j * SWA_BLK:(j + 1) * SWA_BLK, :]
        for t in range(SWA_HKV):
            kk = kv[:, t * LANES:(t + 1) * LANES]
            vv = kv[:, (SWA_HKV + t) * LANES:(SWA_HKV + t + 1) * LANES]
            sink = jnp.where(top, sink_ref[l, 2 * t], sink_ref[l, 2 * t + 1])
            s = _dot_nt(_stack_heads(q2[:, t * LANES:(t + 1) * LANES]), kk) + bias_ref[variant, t]
            m = jnp.maximum(jnp.max(s, axis=-1, keepdims=True), sink)
            p = jnp.exp2(s - m)
            den = jnp.sum(p, axis=-1, keepdims=True) + jnp.exp2(sink - m)
            o = _dot(p.astype(BF16), vv) * (1.0 / den)
            o_ref[j * SWA_BLK:(j + 1) * SWA_BLK, t * LANES:(t + 1) * LANES] = _unstack_heads(o).astype(BF16)


def _swa(swa, l, sink, bias, bsz, seq):
    tq = TOKEN_TILE
    nb_step = tq // SWA_BLK
    assert seq // SWA_BLK >= 2
    return pl.pallas_call(
        functools.partial(_swa_kernel, l=l, nb_total=seq // SWA_BLK, nb_step=nb_step),
        grid=(bsz, seq // tq),
        in_specs=[pl.BlockSpec(memory_space=pltpu.SMEM),
                  pl.BlockSpec((None, tq, 256), lambda b, i: (b, i, 2)),
                  pl.BlockSpec((None, seq, 512), lambda b, i: (b, 0, 0)),
                  _const_spec(bias.shape)],
        out_specs=pl.BlockSpec((None, tq, BRANCH_W), lambda b, i: (b, i, 0)),
        out_shape=jax.ShapeDtypeStruct((bsz, seq, BRANCH_W), BF16),
        compiler_params=_params(("parallel", "parallel")),
        name="swa",
    )(sink, swa, swa, bias)


def _t5_bucket_np(rel):
    nb = T5_BUCKETS // 2
    max_exact = nb // 2
    ret = (rel > 0).astype(np.int32) * nb
    n = np.abs(rel)
    nf = np.maximum(n, 1).astype(np.float32)
    large = max_exact + (np.log(nf / np.float32(max_exact)) / np.float32(math.log(T5_MAX_DIST / max_exact))
                         * np.float32(nb - max_exact)).astype(np.int32)
    large = np.minimum(large, nb - 1)
    return ret + np.where(n < max_exact, n, large)


def _t5_expand_kernel(v_ref, o_ref):
    rows, w = o_ref.shape[0], v_ref.shape[1]
    x = jnp.broadcast_to(v_ref[...], (rows, w))
    o_ref[...] = pltpu.roll(x, 1, 1, stride=1, stride_axis=0)[:, rows:]


def _swa_bias(t5_bias):
    rel = np.arange(-(2 * SWA_BLK - 1), 2 * SWA_BLK)
    rel = np.concatenate([rel, [2 * SWA_BLK]])
    onehot = jnp.asarray(np.eye(T5_BUCKETS, dtype=np.float32)[_t5_bucket_np(rel)])
    vec = jnp.dot(onehot, t5_bias, precision=lax.Precision.HIGHEST).T
    vec = jnp.where(jnp.asarray(np.abs(rel) <= SWA_WIN)[None], vec, NEG)
    w = vec.shape[1]
    tab = pl.pallas_call(
        _t5_expand_kernel,
        grid=(SWA_HQ,),
        in_specs=[pl.BlockSpec((None, 1, w), lambda h: (h, 0, 0))],
        out_specs=pl.BlockSpec((None, SWA_BLK, 3 * SWA_BLK), lambda h: (h, 0, 0)),
        out_shape=jax.ShapeDtypeStruct((SWA_HQ, SWA_BLK, 3 * SWA_BLK), F32),
        name="t5_expand",
    )(vec.reshape(SWA_HQ, 1, w))
    tab = tab.reshape(SWA_HKV, 2 * SWA_BLK, 3 * SWA_BLK) * LOG2E
    col = np.arange(3 * SWA_BLK)
    first = jnp.where(jnp.asarray(col >= SWA_BLK), tab, NEG)
    last = jnp.where(jnp.asarray(col < 2 * SWA_BLK), tab, NEG)
    return jnp.stack([tab, first, last])


def _na_kernel(q_ref, k_ref, v_ref, bias_ref, o_ref, *, rows, groups_step):
    i = pl.program_id(1)
    nq = NA_GROUP * GRID_W
    nkey = NA_WIN * GRID_W
    for gg in range(groups_step):
        r0 = (i * groups_step + gg) * NA_GROUP
        ws = jnp.clip(r0 - NA_KH // 2, 0, rows - NA_WIN)
        variant = (r0 - ws) // NA_GROUP
        ks = pl.multiple_of(ws * GRID_W, GRID_W)
        kt = k_ref[pl.ds(ks, nkey), :]
        vt = v_ref[pl.ds(ks, nkey), :]
        q2 = q_ref[gg * nq:(gg + 1) * nq, :]
        for t in range(NA_H // 2):
            sl = slice(t * LANES, (t + 1) * LANES)
            s = _dot_nt(_stack_heads(q2[:, sl]), kt[:, sl]) + bias_ref[variant, t]
            m = jnp.max(s, axis=-1, keepdims=True)
            p = jnp.exp2(s - m)
            den = jnp.sum(p, axis=-1, keepdims=True)
            o = _dot(p.astype(BF16), vt[:, sl]) * (1.0 / den)
            o_ref[gg * nq:(gg + 1) * nq, sl] = _unstack_heads(o).astype(BF16)


def _na(na, l, bias, bsz, seq):
    rows = seq // GRID_W
    assert rows >= NA_WIN and rows % NA_GROUP == 0
    gstep = NA_GROUPS_PER_STEP
    tq = gstep * NA_GROUP * GRID_W
    return pl.pallas_call(
        functools.partial(_na_kernel, rows=rows, groups_step=gstep),
        grid=(bsz, seq // tq),
        in_specs=[pl.BlockSpec((None, tq, 256), lambda b, i: (b, i, 0)),
                  pl.BlockSpec((None, seq, 256), lambda b, i: (b, 0, 1)),
                  pl.BlockSpec((None, seq, 256), lambda b, i: (b, 0, 2)),
                  _layer_spec(bias, l)],
        out_specs=pl.BlockSpec((None, tq, BRANCH_W), lambda b, i: (b, i, 0)),
        out_shape=jax.ShapeDtypeStruct((bsz, seq, BRANCH_W), BF16),
        compiler_params=_params(("parallel", "parallel")),
        name="na",
    )(na, na, na, bias)


def _na_bias(rpb):
    hp = lax.Precision.HIGHEST
    cols = np.arange(GRID_W)
    cs = np.clip(cols - NA_KW // 2, 0, GRID_W - NA_KW)
    kc = np.arange(GRID_W)
    valid_c = (kc[None, :] >= cs[:, None]) & (kc[None, :] < cs[:, None] + NA_KW)
    dc = np.clip(kc[None, :] - cols[:, None] + (NA_KW - 1), 0, 2 * NA_KW - 2)
    off = np.arange(3)[:, None, None] * NA_GROUP
    rl = np.arange(NA_GROUP)[None, :, None]
    kl = np.arange(NA_WIN)[None, None, :]
    rs = np.stack([np.zeros(NA_GROUP, np.int64), np.arange(NA_GROUP),
                   np.full(NA_GROUP, NA_WIN - NA_KH)])[:, :, None]
    valid_r = (kl >= rs) & (kl < rs + NA_KH)
    dr = np.clip(kl - off - rl + (NA_KH - 1), 0, 2 * NA_KH - 2)
    oh_c = np.eye(2 * NA_KW - 1, dtype=np.float32)[dc] * valid_c[..., None]
    oh_r = np.eye(2 * NA_KH - 1, dtype=np.float32)[dr] * valid_r[..., None]
    tab = jnp.einsum('vrka,hab,cjb->vhrckj', jnp.asarray(oh_r), rpb, jnp.asarray(oh_c), precision=hp)
    valid = valid_r[:, None, :, None, :, None] & valid_c[None, None, None, :, None, :]
    tab = jnp.where(jnp.asarray(valid), tab * LOG2E, NEG)
    return tab.reshape(3, NA_H // 2, 2 * NA_GROUP * GRID_W, NA_WIN * GRID_W)


def _mla_kernel(q_ref, k_ref, v_ref, o_ref, *, seq, tk):
    tq = q_ref.shape[0]

    def body(j, carry):
        ks = pl.multiple_of(j * tk, tk)
        out = []
        for h in range(MLA_H):
            m, l, acc = carry[3 * h:3 * h + 3]
            hs = slice(h * LANES, (h + 1) * LANES)
            ts = slice((h // 2) * LANES, (h // 2 + 1) * LANES)
            s = _dot_nt(q_ref[:, hs], k_ref[pl.ds(ks, tk), hs])
            m_new = jnp.maximum(m, jnp.max(s, axis=-1, keepdims=True))
            alpha = jnp.exp2(m - m_new)
            p = jnp.exp2(s - m_new)
            l = alpha * l + jnp.sum(p, axis=-1, keepdims=True)
            acc = alpha * acc + _dot(p.astype(BF16), v_ref[pl.ds(ks, tk), ts])
            out += [m_new, l, acc]
        return tuple(out)

    init = (jnp.full((tq, 1), -jnp.inf, F32), jnp.zeros((tq, 1), F32), jnp.zeros((tq, LANES), F32)) * MLA_H
    trips = seq // tk
    res = lax.fori_loop(0, trips, body, init, unroll=2 if trips % 2 == 0 else 1)
    lo = _lo_lanes()
    for t in range(MLA_H // 2):
        oa = res[6 * t + 2] * (1.0 / res[6 * t + 1])
        ob = res[6 * t + 5] * (1.0 / res[6 * t + 4])
        o_ref[:, t * LANES:(t + 1) * LANES] = jnp.where(lo, oa, ob).astype(BF16)


def _mla(q, k, v, bsz, seq):
    tq = MLA_TQ
    tk = min(MLA_TK, seq)
    assert seq % tk == 0 and seq % tq == 0
    return pl.pallas_call(
        functools.partial(_mla_kernel, seq=seq, tk=tk),
        grid=(bsz, seq // tq),
        in_specs=[pl.BlockSpec((None, tq, MLA_H * LANES), lambda b, i: (b, i, 0)),
                  pl.BlockSpec((None, seq, MLA_H * LANES), lambda b, i: (b, 0, 0)),
                  pl.BlockSpec((None, seq, BRANCH_W), lambda b, i: (b, 0, 0))],
        out_specs=pl.BlockSpec((None, tq, BRANCH_W), lambda b, i: (b, i, 0)),
        out_shape=jax.ShapeDtypeStruct((bsz, seq, BRANCH_W), BF16),
        compiler_params=_params(("parallel", "parallel")),
        name="mla",
    )(q, k, v)


def _memkv_kernel(m_ref, g_ref, w_ref, o_ref):
    o_ref[...] = _dot(_rms(m_ref[...], g_ref[...]).astype(BF16), w_ref[...]).astype(BF16)


def _memkv(mem, l, g, w):
    bsz, nm, _ = mem.shape
    wd = w.shape[-1]
    return pl.pallas_call(
        _memkv_kernel,
        grid=(bsz,),
        in_specs=[pl.BlockSpec((None, nm, D_MODEL), lambda b: (b, 0, 0)),
                  _layer_spec(g, l), _layer_spec(w, l)],
        out_specs=pl.BlockSpec((None, nm, wd), lambda b: (b, 0, 0)),
        out_shape=jax.ShapeDtypeStruct((bsz, nm, wd), BF16),
        compiler_params=_params(("parallel",)),
        name="memkv",
    )(mem, g, w)


def _memattn_kernel(q_ref, kv_ref, o_ref):
    hd = MEM_H * HEAD_DIM
    for t in range(MEM_H // 2):
        sl = slice(t * LANES, (t + 1) * LANES)
        s = _dot_nt(_stack_heads(q_ref[:, sl]), kv_ref[:, sl])
        m = jnp.max(s, axis=-1, keepdims=True)
        p = jnp.exp2(s - m)
        den = jnp.sum(p, axis=-1, keepdims=True)
        o = _dot(p.astype(BF16), kv_ref[:, hd + t * LANES:hd + (t + 1) * LANES]) * (1.0 / den)
        o_ref[:, sl] = _unstack_heads(o).astype(BF16)


def _memattn(q, kv, bsz, seq):
    tq = TOKEN_TILE
    nm, wd = kv.shape[1], kv.shape[2]
    return pl.pallas_call(
        _memattn_kernel,
        grid=(bsz, seq // tq),
        in_specs=[pl.BlockSpec((None, tq, BRANCH_W), lambda b, i: (b, i, 0)),
                  pl.BlockSpec((None, nm, wd), lambda b, i: (b, 0, 0))],
        out_specs=pl.BlockSpec((None, tq, BRANCH_W), lambda b, i: (b, i, 0)),
        out_shape=jax.ShapeDtypeStruct((bsz, seq, BRANCH_W), BF16),
        compiler_params=_params(("parallel", "parallel")),
        name="memattn",
    )(q, kv)


def _merge_kernel(h_ref, g_ref, b0, b1, b2, b3, b4, wg_ref, wb_ref, wo_ref, o_ref):
    h = h_ref[...]
    u = _rms(h, g_ref[...]).astype(BF16)
    merged = None
    for n, br in enumerate((b0, b1, b2, b3, b4)):
        gate = _sigmoid(_dot(u, wg_ref[:, n * D_MODEL:(n + 1) * D_MODEL]))
        term = gate * _dot(br[...], wb_ref[n])
        merged = term if merged is None else merged + term
    o_ref[...] = h + _dot(merged.astype(BF16), wo_ref[...])


def _merge(h, l, g, branches, wgate, wbr, wout):
    n = h.shape[0]
    tm = TOKEN_TILE
    row = lambda i: (i, 0)
    return pl.pallas_call(
        _merge_kernel,
        grid=(n // tm,),
        in_specs=[pl.BlockSpec((tm, D_MODEL), row), _layer_spec(g, l)]
                 + [pl.BlockSpec((tm, BRANCH_W), row)] * N_BRANCH
                 + [_layer_spec(wgate, l), _layer_spec(wbr, l), _layer_spec(wout, l)],
        out_specs=pl.BlockSpec((tm, D_MODEL), row),
        out_shape=jax.ShapeDtypeStruct((n, D_MODEL), F32),
        compiler_params=_params(("parallel",)),
        name="merge",
    )(h, g, *branches, wgate, wbr, wout)


def _pack_w_in(w):
    offs = np.cumsum((256, 256, 128, 128, 256, 256, 256, MLA_Q_RANK, MLA_KV_RANK, MLA_ROPE, 256))
    a_in, swa_q, swa_k, swa_v, na_q, na_k, na_v, c_q, c_kv, k_rope, mem_q, gates = jnp.split(w, offs, axis=1)
    sc = HEAD_DIM ** -0.5 * LOG2E
    dup = lambda x: jnp.concatenate([x[:, :HEAD_DIM]] * 2 + [x[:, HEAD_DIM:]] * 2, axis=1)
    z = lambda c: jnp.zeros((w.shape[0], c), w.dtype)
    hr = MLA_ROPE // 2
    x1, x2 = k_rope[:, :hr], k_rope[:, hr:]
    kr_a = jnp.concatenate([z(MLA_NOPE), x1, x2, z(LANES - MLA_NOPE - MLA_ROPE)], axis=1)
    kr_b = jnp.concatenate([z(MLA_NOPE), -x2, x1, z(LANES - MLA_NOPE - MLA_ROPE)], axis=1)
    packed = jnp.concatenate([
        dup(swa_k), dup(swa_v), swa_q * sc,
        na_q * sc, na_k, na_v,
        mem_q * sc,
        c_q, z(256 - MLA_Q_RANK), c_kv, kr_a, kr_b], axis=1)
    assert packed.shape[1] == IN_COLS
    return a_in.T.astype(BF16), packed.astype(BF16), gates.astype(BF16)


def _pack_mla(w_q_up, w_kv_up):
    dq = MLA_NOPE + MLA_ROPE
    hr = MLA_ROPE // 2
    zq = lambda c: jnp.zeros((MLA_Q_RANK, c), w_q_up.dtype)
    qa, qb = [], []
    for h in range(MLA_H):
        blk = w_q_up[:, h * dq:(h + 1) * dq]
        nope, x1, x2 = blk[:, :MLA_NOPE], blk[:, MLA_NOPE:MLA_NOPE + hr], blk[:, MLA_NOPE + hr:]
        qa.append(jnp.concatenate([nope, x1, x2, zq(LANES - dq)], axis=1))
        qb.append(jnp.concatenate([zq(MLA_NOPE), -x2, x1, zq(LANES - dq)], axis=1))
    wq = jnp.concatenate(qa + qb, axis=1)
    wq = jnp.pad(wq, ((0, 256 - MLA_Q_RANK), (0, 0)))
    zk = jnp.zeros((MLA_KV_RANK, LANES - MLA_NOPE), w_kv_up.dtype)
    ks, vs = [], []
    for h in range(MLA_H):
        blk = w_kv_up[:, h * (MLA_NOPE + MLA_V):(h + 1) * (MLA_NOPE + MLA_V)]
        ks.append(jnp.concatenate([blk[:, :MLA_NOPE], zk], axis=1))
        vs.append(blk[:, MLA_NOPE:])
    wkv = jnp.concatenate(ks + vs, axis=1)
    return wq.astype(BF16), wkv.astype(BF16)


def _rope_tables(seq):
    half = MLA_ROPE // 2
    inv = ROPE_THETA ** (-jnp.arange(half, dtype=F32) / half)
    ang = jnp.arange(seq, dtype=F32)[:, None] * inv[None, :]
    cos, sin = jnp.cos(ang), jnp.sin(ang)
    one = jnp.ones((seq, MLA_NOPE), F32)
    z0 = jnp.zeros((seq, MLA_NOPE), F32)
    zt = jnp.zeros((seq, LANES - MLA_NOPE - MLA_ROPE), F32)
    return (jnp.concatenate([one, cos, cos, zt], axis=1),
            jnp.concatenate([z0, sin, sin, zt], axis=1))


def kernel(x_prompt, x_sample, mem_prompt, mem_sample,
           ffn1_norm, ffn1_w_gate, ffn1_w_up, ffn1_w_down,
           mix_norm, w_in,
           ssm_lam_re, ssm_lam_im, ssm_log_step, ssm_b_re, ssm_b_im, ssm_c_re, ssm_c_im,
           ssm_d, ssm_w_glu,
           swa_sink, t5_bias, na_rpb,
           mla_q_norm, mla_w_q_up, mla_kv_norm, mla_w_kv_up,
           mem_norm, mem_w_kv, w_branch, w_out,
           ffn2_norm, ffn2_w_gate, ffn2_w_up, ffn2_w_down,
           final_norm):
    groups = []
    for x, mem in ((x_prompt, mem_prompt), (x_sample, mem_sample)):
        bsz, seq, _ = x.shape
        assert seq % TOKEN_TILE == 0 and seq % (NA_GROUPS_PER_STEP * NA_GROUP * GRID_W) == 0
        assert seq % (SSM_T * SUBLANES) == 0
        groups.append(dict(b=bsz, s=seq, h=x.reshape(bsz * seq, D_MODEL), mem=mem))
    max_seq = max(g['s'] for g in groups)
    cos_t, sin_t = _rope_tables(max_seq)
    swa_bias = _swa_bias(t5_bias.astype(F32))

    segments, base = [], 0
    for g in groups:
        for _ in range(g['b']):
            segments.append((base, g['s'] // SSM_T))
            base += g['s'] // SSM_T
    segments = tuple(segments)

    nl = w_in.shape[0]
    rows3 = lambda v: v.reshape(nl, 1, -1)
    bf = lambda v: v.astype(BF16)
    wa_t, w_pack, w_gate = jax.vmap(_pack_w_in)(w_in)
    wq, wkv = jax.vmap(_pack_mla)(mla_w_q_up, mla_w_kv_up)
    qn = rows3(jnp.pad(mla_q_norm, ((0, 0), (0, 256 - MLA_Q_RANK))))
    na_bias = jax.vmap(_na_bias)(na_rpb.astype(F32))
    vtab, win, wout, apow = jax.vmap(_ssm_tables)(ssm_lam_re, ssm_lam_im, ssm_log_step,
                                                  ssm_b_re, ssm_b_im, ssm_c_re, ssm_c_im)
    lg = lambda v: v.reshape((nl * SSM_G,) + v.shape[2:])
    mt = _ssm_toeplitz(lg(vtab))
    win, wout, apow = lg(win), lg(wout), lg(apow)
    f1 = (rows3(ffn1_norm), bf(ffn1_w_gate), bf(ffn1_w_up), bf(ffn1_w_down))
    f2 = (rows3(ffn2_norm), bf(ffn2_w_gate), bf(ffn2_w_up), bf(ffn2_w_down))
    mixn, kvn, memn = rows3(mix_norm), rows3(mla_kv_norm), rows3(mem_norm)
    wbr, wo, wmem = bf(w_branch), bf(w_out), bf(mem_w_kv)
    wglu_t = bf(ssm_w_glu.transpose(0, 2, 1))
    ssm_d3 = ssm_d.reshape(nl, BRANCH_W, 1)
    sink = swa_sink.astype(F32) * LOG2E
    final_g = final_norm.reshape(1, -1)

    for l in range(DEPTH):
        for g in groups:
            g['h'] = _ffn(g['h'], l, *f1)
            (g['a'], g['swa'], g['na'], g['mq'], g['q'], g['k'], g['v']) = _inproj(
                g['h'], g['s'], l, mixn, wa_t, w_pack, wq, wkv, qn, kvn, cos_t, sin_t)

        u4 = [g['a'].reshape(SSM_G, SSM_P, -1, SSM_T) for g in groups]
        y4 = _ssm_conv(u4, l, mt, win, wout, apow, segments)

        for g, y in zip(groups, y4):
            bsz, seq = g['b'], g['s']
            n = bsz * seq
            r3 = lambda v: v.reshape(bsz, seq, v.shape[-1])
            br0 = _ssm_post(g['a'], y.reshape(BRANCH_W, n), l, ssm_d3, wglu_t)
            br1 = _swa(r3(g['swa']), l, sink, swa_bias, bsz, seq)
            br2 = _na(r3(g['na']), l, na_bias, bsz, seq)
            br3 = _mla(r3(g['q']), r3(g['k']), r3(g['v']), bsz, seq)
            kvm = _memkv(g['mem'], l, memn, wmem)
            br4 = _memattn(r3(g['mq']), kvm, bsz, seq)
            flat = lambda v: v.reshape(n, BRANCH_W)
            g['h'] = _merge(g['h'], l, mixn,
                            (br0, flat(br1), flat(br2), flat(br3), flat(br4)),
                            w_gate, wbr, wo)
            g['h'] = _ffn(g['h'], l, *f2, final_g=final_g if l == DEPTH - 1 else None)

    return tuple(g['h'].reshape(g['b'], g['s'], D_MODEL) for g in groups)
```
